```python
import math
import jax, jax.numpy as jnp
from jax import lax
import numpy as np

D_MODEL = 1024
BATCH = 8
SEQ = 2048
DEPTH = 4
DEC_BATCH = 128
DEC_SEQ = 1
PAST_LEN = 16384
PAGE_SIZE = 128

D_SSM = D_MODEL // 2
GROUP = 16
N_GROUPS = D_SSM // GROUP
P_STATE = 64
D_CONV = D_MODEL // 2
CONV_W = 31
D_FF = -(-8 * D_MODEL // (3 * 256)) * 256
D_IN = D_SSM + 2 * D_CONV + 2 * D_MODEL
DT_MIN = 1e-3
DT_MAX = 1e-1
EPS = 1e-6

kernel_name = "s5_conformer_gated_hybrid_step"


def rmsnorm(x, g):
    xf = x.astype(jnp.float32)
    y = xf * lax.rsqrt(jnp.mean(xf * xf, axis=-1, keepdims=True) + EPS)
    return (y * g.astype(jnp.float32)).astype(x.dtype)


def layernorm(x, g, b):
    xf = x.astype(jnp.float32)
    mu = jnp.mean(xf, axis=-1, keepdims=True)
    var = jnp.mean(jnp.square(xf - mu), axis=-1, keepdims=True)
    y = (xf - mu) * lax.rsqrt(var + EPS)
    return (y * g.astype(jnp.float32) + b.astype(jnp.float32)).astype(x.dtype)


def _cmul_combine(e1, e2):
    a1r, a1i, b1r, b1i = e1
    a2r, a2i, b2r, b2i = e2
    return (a2r * a1r - a2i * a1i,
            a2r * a1i + a2i * a1r,
            a2r * b1r - a2i * b1i + b2r,
            a2r * b1i + a2i * b1r + b2i)


def s5_layer(u, h0_re, h0_im, a_re, a_im, log_dt, b_re, b_im, c_re, c_im, d):
    f32 = jnp.float32
    bsz, seqlen = u.shape[0], u.shape[1]
    a_re, a_im = a_re.astype(f32), a_im.astype(f32)
    dt = jnp.exp(log_dt.astype(f32))[:, None]
    mag = jnp.exp(dt * a_re)
    ab_re, ab_im = mag * jnp.cos(dt * a_im), mag * jnp.sin(dt * a_im)
    nr, ni = ab_re - 1.0, ab_im
    den = a_re * a_re + a_im * a_im
    q_re = (nr * a_re + ni * a_im) / den
    q_im = (ni * a_re - nr * a_im) / den
    b_re, b_im = b_re.astype(f32), b_im.astype(f32)
    bb_re = q_re[..., None] * b_re - q_im[..., None] * b_im
    bb_im = q_re[..., None] * b_im + q_im[..., None] * b_re
    ug = u.astype(f32).reshape(bsz, seqlen, N_GROUPS, GROUP)
    bu_re = jnp.einsum('blgc,gpc->blgp', ug, bb_re)
    bu_im = jnp.einsum('blgc,gpc->blgp', ug, bb_im)
    h0r, h0i = h0_re.astype(f32), h0_im.astype(f32)
    bu_re = bu_re.at[:, 0].add(ab_re * h0r - ab_im * h0i)
    bu_im = bu_im.at[:, 0].add(ab_re * h0i + ab_im * h0r)
    el_re = jnp.broadcast_to(ab_re, bu_re.shape)
    el_im = jnp.broadcast_to(ab_im, bu_im.shape)
    _, _, h_re, h_im = lax.associative_scan(_cmul_combine, (el_re, el_im, bu_re, bu_im), axis=1)
    y = (jnp.einsum('blgp,gcp->blgc', h_re, c_re.astype(f32))
         - jnp.einsum('blgp,gcp->blgc', h_im, c_im.astype(f32)))
    y = y.reshape(bsz, seqlen, D_SSM) + d.astype(f32) * u.astype(f32)
    return y.astype(u.dtype), h_re[:, -1].astype(h0_re.dtype), h_im[:, -1].astype(h0_im.dtype)


def causal_dwconv(v, buf, w, b):
    full = jnp.concatenate([buf.astype(v.dtype), v], axis=1)
    out = lax.conv_general_dilated(
        full, w.astype(v.dtype)[:, None, :], window_strides=(1,), padding='VALID',
        dimension_numbers=('NWC', 'WIO', 'NWC'), feature_group_count=v.shape[-1])
    return out + b, full[:, -(CONV_W - 1):]


def hybrid_layer(x, h_re, h_im, cbuf, p):
    (g_mix, w_in, a_re, a_im, log_dt, b_re, b_im, c_re, c_im, d, w_glu,
     conv_w, conv_b, ln_g, ln_b, w_pw, w_out, g_ffn, w_ff_in, w_ff_out) = p
    xn = rmsnorm(x, g_mix)
    z = xn @ w_in
    u = z[..., :D_SSM]
    cv = z[..., D_SSM:D_SSM + D_CONV]
    cg = z[..., D_SSM + D_CONV:D_SSM + 2 * D_CONV]
    gates = jax.nn.sigmoid(z[..., D_SSM + 2 * D_CONV:])
    y, hr, hi = s5_layer(u, h_re, h_im, a_re, a_im, log_dt, b_re, b_im, c_re, c_im, d)
    yg = jax.nn.gelu(y) @ w_glu
    ya = yg[..., :D_MODEL] * jax.nn.sigmoid(yg[..., D_MODEL:])
    c = cv * jax.nn.sigmoid(cg)
    c, new_buf = causal_dwconv(c, cbuf, conv_w, conv_b)
    c = jax.nn.silu(layernorm(c, ln_g, ln_b))
    yb = c @ w_pw
    merged = gates[..., :D_MODEL] * ya + gates[..., D_MODEL:] * yb
    x = x + merged @ w_out
    hn = rmsnorm(x, g_ffn) @ w_ff_in
    x = x + (jax.nn.silu(hn[..., :D_FF]) * hn[..., D_FF:]) @ w_ff_out
    return x, hr, hi, new_buf


def run_trunk(x, s_re, s_im, s_conv, stacked, g_final):
    new_re, new_im, new_conv = [], [], []
    for l in range(DEPTH):
        p = tuple(w[l] for w in stacked)
        x, hr, hi, nb = hybrid_layer(x, s_re[l], s_im[l], s_conv[l], p)
        new_re.append(hr)
        new_im.append(hi)
        new_conv.append(nb)
    return rmsnorm(x, g_final), jnp.stack(new_re), jnp.stack(new_im), jnp.stack(new_conv)


def setup_inputs(seed: int = 0) -> dict:
    key = jax.random.key(seed)
    ks = jax.random.split(key, 32)
    f32 = jnp.float32
    nrm = lambda k, shape, s: jax.random.normal(k, shape, f32) * s
    x_prompt = nrm(ks[0], (BATCH, SEQ, D_MODEL), 1.0)
    x_sample = nrm(ks[1], (DEC_BATCH, DEC_SEQ, D_MODEL), 1.0)
    state_ssm_re = nrm(ks[2], (DEPTH, DEC_BATCH, N_GROUPS, P_STATE), 0.5)
    state_ssm_im = nrm(ks[3], (DEPTH, DEC_BATCH, N_GROUPS, P_STATE), 0.5)
    state_conv = nrm(ks[4], (DEPTH, DEC_BATCH, CONV_W - 1, D_CONV), 0.5)
    n_idx = jnp.arange(P_STATE, dtype=f32)
    ssm_a_re = -0.5 + nrm(ks[5], (DEPTH, N_GROUPS, P_STATE), 0.01)
    ssm_a_im = math.pi * n_idx + nrm(ks[6], (DEPTH, N_GROUPS, P_STATE), 0.01)
    ssm_log_dt = jax.random.uniform(ks[7], (DEPTH, N_GROUPS), f32,
                                    math.log(DT_MIN), math.log(DT_MAX))
    bs = (2.0 * GROUP) ** -0.5
    cs = (2.0 * P_STATE) ** -0.5
    return {
        "x_prompt": x_prompt,
        "x_sample": x_sample,
        "state_ssm_re": state_ssm_re,
        "state_ssm_im": state_ssm_im,
        "state_conv": state_conv,
        "g_mix": 1.0 + nrm(ks[8], (DEPTH, D_MODEL), 0.01),
        "w_in": nrm(ks[9], (DEPTH, D_MODEL, D_IN), D_MODEL ** -0.5),
        "ssm_a_re": ssm_a_re,
        "ssm_a_im": ssm_a_im,
        "ssm_log_dt": ssm_log_dt,
        "ssm_b_re": nrm(ks[10], (DEPTH, N_GROUPS, P_STATE, GROUP), bs),
        "ssm_b_im": nrm(ks[11], (DEPTH, N_GROUPS, P_STATE, GROUP), bs),
        "ssm_c_re": nrm(ks[12], (DEPTH, N_GROUPS, GROUP, P_STATE), cs),
        "ssm_c_im": nrm(ks[13], (DEPTH, N_GROUPS, GROUP, P_STATE), cs),
        "ssm_d": nrm(ks[14], (DEPTH, D_SSM), 1.0),
        "w_glu": nrm(ks[15], (DEPTH, D_SSM, 2 * D_MODEL), D_SSM ** -0.5),
        "conv_w": nrm(ks[16], (DEPTH, CONV_W, D_CONV), CONV_W ** -0.5),
        "conv_b": nrm(ks[17], (DEPTH, D_CONV), 0.02),
        "conv_ln_g": 1.0 + nrm(ks[18], (DEPTH, D_CONV), 0.01),
        "conv_ln_b": nrm(ks[19], (DEPTH, D_CONV), 0.02),
        "w_pw": nrm(ks[20], (DEPTH, D_CONV, D_MODEL), D_CONV ** -0.5),
        "w_out": nrm(ks[21], (DEPTH, D_MODEL, D_MODEL), D_MODEL ** -0.5),
        "g_ffn": 1.0 + nrm(ks[22], (DEPTH, D_MODEL), 0.01),
        "w_ff_in": nrm(ks[23], (DEPTH, D_MODEL, 2 * D_FF), D_MODEL ** -0.5),
        "w_ff_out": nrm(ks[24], (DEPTH, D_FF, D_MODEL), D_FF ** -0.5),
        "g_final": 1.0 + nrm(ks[25], (D_MODEL,), 0.01),
    }


def reference(x_prompt, x_sample, state_ssm_re, state_ssm_im, state_conv,
              g_mix, w_in, ssm_a_re, ssm_a_im, ssm_log_dt, ssm_b_re, ssm_b_im,
              ssm_c_re, ssm_c_im, ssm_d, w_glu, conv_w, conv_b, conv_ln_g, conv_ln_b,
              w_pw, w_out, g_ffn, w_ff_in, w_ff_out, g_final):
    stacked = (g_mix, w_in, ssm_a_re, ssm_a_im, ssm_log_dt, ssm_b_re, ssm_b_im,
               ssm_c_re, ssm_c_im, ssm_d, w_glu, conv_w, conv_b, conv_ln_g, conv_ln_b,
               w_pw, w_out, g_ffn, w_ff_in, w_ff_out)
    bp = x_prompt.shape[0]
    zero_re = jnp.zeros((DEPTH, bp, N_GROUPS, P_STATE), state_ssm_re.dtype)
    zero_im = jnp.zeros((DEPTH, bp, N_GROUPS, P_STATE), state_ssm_im.dtype)
    zero_conv = jnp.zeros((DEPTH, bp, CONV_W - 1, D_CONV), x_prompt.dtype)
    y_prompt, re_p, im_p, conv_p = run_trunk(x_prompt, zero_re, zero_im, zero_conv, stacked, g_final)
    y_sample, re_s, im_s, conv_s = run_trunk(x_sample, state_ssm_re, state_ssm_im, state_conv, stacked, g_final)
    return (y_prompt, y_sample, re_p, im_p, conv_p, re_s, im_s, conv_s)
```

```python
import functools

import numpy as np
import jax
import jax.numpy as jnp
from jax import lax
from jax.experimental import pallas as pl
from jax.experimental.pallas import tpu as pltpu

D_MODEL = 1024
DEPTH = 4
D_SSM = 512
GROUP = 16
N_GROUPS = 32
P_STATE = 64
D_CONV = 512
CONV_W = 31
D_FF = 2816
EPS = 1e-6

LANES = 128
SUBLANES = 8
MXU = 256
LC = 16
SLOTS = LANES // GROUP
STATE_W = 2 * P_STATE
HIST = 32
HIST_PAD = HIST - (CONV_W - 1)
FF_CHUNK = MXU
VMEM_LIMIT = 56 * 1024 * 1024

SEQ_TT = 128
TOK_TM = 256
CONV_TS = 32

_BF = jnp.bfloat16
_F32 = jnp.float32


def _dot(a, b):
    return jnp.dot(a, b, preferred_element_type=_F32)


def _rms(xf, g):
    return xf * lax.rsqrt(jnp.mean(xf * xf, axis=-1, keepdims=True) + EPS) * g


def _sigmoid(x):
    return jax.nn.sigmoid(x)


def _layernorm_silu(c, g, b):
    mu = jnp.mean(c, axis=-1, keepdims=True)
    var = jnp.mean(jnp.square(c - mu), axis=-1, keepdims=True)
    y = (c - mu) * lax.rsqrt(var + EPS) * g + b
    return y * _sigmoid(y)


def _s5_params(a_re, a_im, log_dt, b_re, b_im, c_re, c_im):
    hi = lax.Precision.HIGHEST
    a_re, a_im = a_re.astype(_F32), a_im.astype(_F32)
    dt = jnp.exp(log_dt.astype(_F32))[..., None]
    lre, lim = dt * a_re, dt * a_im
    n = jnp.arange(LC + 1, dtype=_F32)[None, None, :, None]
    pmag = jnp.exp(n * lre[:, :, None, :])
    pw_re = pmag * jnp.cos(n * lim[:, :, None, :])
    pw_im = pmag * jnp.sin(n * lim[:, :, None, :])
    ab_re, ab_im = pw_re[:, :, 1], pw_im[:, :, 1]
    nr, ni = ab_re - 1.0, ab_im
    den = a_re * a_re + a_im * a_im
    q_re = (nr * a_re + ni * a_im) / den
    q_im = (ni * a_re - nr * a_im) / den
    b_re, b_im = b_re.astype(_F32), b_im.astype(_F32)
    bb_re = q_re[..., None] * b_re - q_im[..., None] * b_im
    bb_im = q_re[..., None] * b_im + q_im[..., None] * b_re
    c_re, c_im = c_re.astype(_F32), c_im.astype(_F32)
    ca_re = c_re[:, :, None] * pw_re[:, :, :, None, :] - c_im[:, :, None] * pw_im[:, :, :, None, :]
    ca_im = c_re[:, :, None] * pw_im[:, :, :, None, :] + c_im[:, :, None] * pw_re[:, :, :, None, :]
    kern = (jnp.einsum('lgdcp,lgpe->lgdce', ca_re[:, :, :LC], bb_re, precision=hi)
            - jnp.einsum('lgdcp,lgpe->lgdce', ca_im[:, :, :LC], bb_im, precision=hi))

    g8 = np.arange(N_GROUPS)[:, None, None] % SLOTS
    half = np.arange(LC // SLOTS)[None, :, None]
    slot = np.arange(SLOTS)[None, None, :]
    j_in = half * SLOTS + (slot - g8) % SLOTS
    j_out = half * SLOTS + (g8 - slot) % SLOTS
    gi = np.arange(N_GROUPS)

    d = j_out[:, None, None, :, :] - j_in[:, :, :, None, None]
    toe = kern[:, gi[:, None, None, None, None], np.clip(d, 0, LC - 1)]
    toe = jnp.where((d >= 0)[None, ..., None, None], toe, 0.0)
    toe = toe.transpose(0, 1, 2, 3, 7, 4, 5, 6).reshape(DEPTH, N_GROUPS, MXU, MXU)

    pin_re = pw_re[:, gi[:, None, None], LC - 1 - j_in]
    pin_im = pw_im[:, gi[:, None, None], LC - 1 - j_in]
    bt_re = bb_re.transpose(0, 1, 3, 2)[:, :, None, None]
    bt_im = bb_im.transpose(0, 1, 3, 2)[:, :, None, None]
    win_re = pin_re[:, :, :, :, None, :] * bt_re - pin_im[:, :, :, :, None, :] * bt_im
    win_im = pin_re[:, :, :, :, None, :] * bt_im + pin_im[:, :, :, :, None, :] * bt_re
    w_sin = jnp.stack([win_re, win_im], axis=-2).reshape(DEPTH, N_GROUPS, MXU, STATE_W)

    co_re = ca_re[:, gi[:, None, None], j_out + 1]
    co_im = ca_im[:, gi[:, None, None], j_out + 1]
    w_sout = jnp.stack([co_re, -co_im], axis=2)
    w_sout = w_sout.transpose(0, 1, 2, 6, 3, 4, 5).reshape(DEPTH, N_GROUPS, STATE_W, MXU)

    a_lc_re, a_lc_im = pw_re[:, :, LC], pw_im[:, :, LC]
    m1 = jnp.concatenate([a_lc_re, a_lc_re], axis=-1).reshape(DEPTH, 1, N_GROUPS * STATE_W)
    m2 = jnp.concatenate([-a_lc_im, a_lc_im], axis=-1).reshape(DEPTH, 1, N_GROUPS * STATE_W)
    a_lc = jnp.concatenate([m1, m2], axis=1)

    eye = jnp.eye(N_GROUPS, dtype=_F32)
    bd_re = jnp.einsum('lgpc,gh->lgchp', bb_re, eye).reshape(DEPTH, D_SSM, N_GROUPS * P_STATE)
    bd_im = jnp.einsum('lgpc,gh->lgchp', bb_im, eye).reshape(DEPTH, D_SSM, N_GROUPS * P_STATE)
    cd_re = jnp.einsum('lgcp,gh->lgphc', c_re, eye).reshape(DEPTH, N_GROUPS * P_STATE, D_SSM)
    cd_im = jnp.einsum('lgcp,gh->lgphc', -c_im, eye).reshape(DEPTH, N_GROUPS * P_STATE, D_SSM)
    abar = jnp.stack([ab_re.reshape(DEPTH, -1), ab_im.reshape(DEPTH, -1)], axis=1)
    return dict(toe=toe.astype(_BF), w_sin=w_sin.astype(_BF), w_sout=w_sout.astype(_BF), a_lc=a_lc,
                bd_re=bd_re.astype(_BF), bd_im=bd_im.astype(_BF),
                cd_re=cd_re.astype(_BF), cd_im=cd_im.astype(_BF), abar=abar)


def _seq_prompt_kernel(x_ref, gmix_ref, wu_ref, toe_ref, wsin_ref, wsout_ref, alc_ref, d_ref,
                       gy_ref, hfin_ref,
                       u_scr, lhs_scr, e_scr, hs_scr, yg_scr, y_scr, hc_scr, *, tt):
    nb = x_ref.shape[0]
    rows = nb * tt
    m = rows // LC
    nk = tt // LC

    @pl.when(pl.program_id(0) == 0)
    def _():
        hc_scr[...] = jnp.zeros_like(hc_scr)

    x = x_ref[...].reshape(rows, D_MODEL)
    xn = _rms(x, gmix_ref[...]).astype(_BF)
    u = _dot(xn, wu_ref[...])
    for v in range(D_SSM // LANES):
        u_scr[v] = u[:, LANES * v:LANES * (v + 1)]

    slot = lax.broadcasted_iota(jnp.int32, (m, LANES), 1) // GROUP

    for v in range(D_SSM // LANES):
        rolled = []
        for j in range(LC):
            piece = u_scr[v, pl.ds(j, m, stride=LC), :]
            r = j % SLOTS
            rolled.append(pltpu.roll(piece, GROUP * r, 1) if r else piece)
        for gl in range(SLOTS):
            halves = []
            for h in range(LC // SLOTS):
                acc = rolled[SLOTS * h]
                for jl in range(1, SLOTS):
                    acc = jnp.where(slot == (gl + jl) % SLOTS, rolled[SLOTS * h + jl], acc)
                halves.append(acc)
            lhs_scr[SLOTS * v + gl] = jnp.concatenate(halves, axis=1).astype(_BF)

    for g in range(N_GROUPS):
        e_scr[g] = _dot(lhs_scr[g], wsin_ref[g])

    for g in range(N_GROUPS):
        sl = pl.ds(STATE_W * g, STATE_W)
        m1 = alc_ref[0:1, sl]
        m2 = alc_ref[1:2, sl]
        h = hc_scr[:, sl]
        for k in range(nk):
            hs_scr[g, pl.ds(k, nb, stride=nk), :] = h
            e = e_scr[g, pl.ds(k, nb, stride=nk), :]
            h = h * m1 + pltpu.roll(h, P_STATE, 1) * m2 + e
        hc_scr[:, sl] = h
        hfin_ref[:, sl] = h

    for g in range(N_GROUPS):
        yg_scr[g] = _dot(lhs_scr[g], toe_ref[g]) + _dot(hs_scr[g].astype(_BF), wsout_ref[g])

    for v in range(D_SSM // LANES):
        dv = d_ref[:, pl.ds(LANES * v, LANES)]
        for j in range(LC):
            hf, jl = divmod(j, SLOTS)
            acc = yg_scr[SLOTS * v, :, pl.ds(LANES * hf, LANES)]
            for gl in range(1, SLOTS):
                acc = jnp.where(slot == (gl - jl) % SLOTS,
                                yg_scr[SLOTS * v + gl, :, pl.ds(LANES * hf, LANES)], acc)
            if jl:
                acc = pltpu.roll(acc, GROUP * jl, 1)
            y_scr[v, pl.ds(j, m, stride=LC), :] = acc + dv * u_scr[v, pl.ds(j, m, stride=LC), :]

    y = jnp.concatenate([y_scr[v] for v in range(D_SSM // LANES)], axis=1)
    gy_ref[...] = jax.nn.gelu(y).astype(_BF).reshape(nb, tt, D_SSM)


def _seq_prompt(x, l, gmix, wu, s5, ssm_d):
    nb, seq, _ = x.shape
    tt = SEQ_TT
    rows = nb * tt
    m = rows // LC
    const = lambda shape: pl.BlockSpec((None,) + shape, lambda i: (l,) + (0,) * len(shape))
    return pl.pallas_call(
        functools.partial(_seq_prompt_kernel, tt=tt),
        grid=(seq // tt,),
        in_specs=[
            pl.BlockSpec((nb, tt, D_MODEL), lambda i: (0, i, 0)),
            const((1, D_MODEL)),
            const((D_MODEL, D_SSM)),
            const((N_GROUPS, MXU, MXU)),
            const((N_GROUPS, MXU, STATE_W)),
            const((N_GROUPS, STATE_W, MXU)),
            const((2, N_GROUPS * STATE_W)),
            const((1, D_SSM)),
        ],
        out_specs=[
            pl.BlockSpec((nb, tt, D_SSM), lambda i: (0, i, 0)),
            pl.BlockSpec((nb, N_GROUPS * STATE_W), lambda i: (0, 0)),
        ],
        out_shape=[
            jax.ShapeDtypeStruct((nb, seq, D_SSM), _BF),
            jax.ShapeDtypeStruct((nb, N_GROUPS * STATE_W), _F32),
        ],
        scratch_shapes=[
            pltpu.VMEM((D_SSM // LANES, rows, LANES), _F32),
            pltpu.VMEM((N_GROUPS, m, MXU), _BF),
            pltpu.VMEM((N_GROUPS, m, STATE_W), _F32),
            pltpu.VMEM((N_GROUPS, m, STATE_W), _F32),
            pltpu.VMEM((N_GROUPS, m, MXU), _F32),
            pltpu.VMEM((D_SSM // LANES, rows, LANES), _F32),
            pltpu.VMEM((nb, N_GROUPS * STATE_W), _F32),
        ],
        compiler_params=pltpu.CompilerParams(
            dimension_semantics=("arbitrary",), vmem_limit_bytes=VMEM_LIMIT),
        name=f"seq_prompt_l{l}",
    )(x, gmix, wu, s5['toe'], s5['w_sin'], s5['w_sout'], s5['a_lc'], ssm_d)


def _seq_sample_kernel(x_ref, hre_ref, him_ref, gmix_ref, wu_ref, bdre_ref, bdim_ref,
                       cdre_ref, cdim_ref, abar_ref, d_ref, gy_ref, nre_ref, nim_ref):
    xn = _rms(x_ref[...], gmix_ref[...]).astype(_BF)
    u = _dot(xn, wu_ref[...])
    ub = u.astype(_BF)
    n_state = N_GROUPS * P_STATE
    per_tile = MXU // P_STATE
    bu_re, bu_im = [], []
    for t in range(n_state // MXU):
        kb = (t * per_tile * GROUP) // MXU
        lhs = ub[:, MXU * kb:MXU * (kb + 1)]
        bu_re.append(_dot(lhs, bdre_ref[pl.ds(MXU * kb, MXU), pl.ds(MXU * t, MXU)]))
        bu_im.append(_dot(lhs, bdim_ref[pl.ds(MXU * kb, MXU), pl.ds(MXU * t, MXU)]))
    bu_re = jnp.concatenate(bu_re, axis=1)
    bu_im = jnp.concatenate(bu_im, axis=1)
    ar, ai = abar_ref[0:1, :], abar_ref[1:2, :]
    h0r, h0i = hre_ref[...], him_ref[...]
    hr = ar * h0r - ai * h0i + bu_re
    hi = ar * h0i + ai * h0r + bu_im
    nre_ref[...] = hr
    nim_ref[...] = hi
    hrb, hib = hr.astype(_BF), hi.astype(_BF)
    ys = []
    k_per = (MXU // GROUP) * P_STATE
    for t in range(D_SSM // MXU):
        ks = slice(k_per * t, k_per * (t + 1))
        ys.append(_dot(hrb[:, ks], cdre_ref[pl.ds(k_per * t, k_per), pl.ds(MXU * t, MXU)])
                  + _dot(hib[:, ks], cdim_ref[pl.ds(k_per * t, k_per), pl.ds(MXU * t, MXU)]))
    y = jnp.concatenate(ys, axis=1) + d_ref[...] * u
    gy_ref[...] = jax.nn.gelu(y).astype(_BF)


def _seq_sample(x, h_re, h_im, l, gmix, wu, s5, ssm_d):
    nb = x.shape[0]
    n_state = N_GROUPS * P_STATE
    const = lambda shape: pl.BlockSpec((None,) + shape, lambda i: (l,) + (0,) * len(shape))
    full = lambda shape: pl.BlockSpec(shape, lambda i: (0,) * len(shape))
    return pl.pallas_call(
        _seq_sample_kernel,
        grid=(1,),
        in_specs=[
            full((nb, D_MODEL)), const((nb, n_state)), const((nb, n_state)),
            const((1, D_MODEL)), const((D_MODEL, D_SSM)),
            const((D_SSM, n_state)), const((D_SSM, n_state)),
            const((n_state, D_SSM)), const((n_state, D_SSM)),
            const((2, n_state)), const((1, D_SSM)),
        ],
        out_specs=[full((nb, D_SSM)), full((nb, n_state)), full((nb, n_state))],
        out_shape=[
            jax.ShapeDtypeStruct((nb, D_SSM), _BF),
            jax.ShapeDtypeStruct((nb, n_state), _F32),
            jax.ShapeDtypeStruct((nb, n_state), _F32),
        ],
        compiler_params=pltpu.CompilerParams(
            dimension_semantics=("arbitrary",), vmem_limit_bytes=VMEM_LIMIT),
        name=f"seq_sample_l{l}",
    )(x, h_re, h_im, gmix, wu, s5['bd_re'], s5['bd_im'], s5['cd_re'], s5['cd_im'], s5['abar'], ssm_d)


def _merge_and_ffn(x, xn, cact, gy, wrest_ref, wglu_ref, wpw_ref, wout_ref, gffn_ref,
                   wffi_ref, wffo_ref, gfin_ref, final):
    yg = _dot(gy, wglu_ref[...])
    ya = yg[:, :D_MODEL] * _sigmoid(yg[:, D_MODEL:])
    yb = _dot(cact, wpw_ref[...])
    ga = _sigmoid(_dot(xn, wrest_ref[:, pl.ds(2 * D_CONV, D_MODEL)]))
    gb = _sigmoid(_dot(xn, wrest_ref[:, pl.ds(2 * D_CONV + D_MODEL, D_MODEL)]))
    merged = (ga * ya + gb * yb).astype(_BF)
    x1 = x + _dot(merged, wout_ref[...])
    hn = _rms(x1, gffn_ref[...]).astype(_BF)
    acc = x1
    for ck in range(D_FF // FF_CHUNK):
        h1 = _dot(hn, wffi_ref[:, pl.ds(FF_CHUNK * ck, FF_CHUNK)])
        h2 = _dot(hn, wffi_ref[:, pl.ds(D_FF + FF_CHUNK * ck, FF_CHUNK)])
        a = (h1 * _sigmoid(h1) * h2).astype(_BF)
        acc = acc + _dot(a, wffo_ref[pl.ds(FF_CHUNK * ck, FF_CHUNK), :])
    if final:
        acc = _rms(acc, gfin_ref[...])
    return acc


def _tok_prompt_kernel(x_ref, gy_ref, gmix_ref, wrest_ref, wglu_ref, cw_ref, cb_ref, lng_ref, lnb_ref,
                       wpw_ref, wout_ref, gffn_ref, wffi_ref, wffo_ref, gfin_ref,
                       o_ref, hist_ref, cfull_scr, cact_scr, *, tm, final):
    @pl.when(pl.program_id(1) == 0)
    def _():
        cfull_scr[pl.ds(0, HIST), :] = jnp.zeros((HIST, D_CONV), _F32)

    x = x_ref[...]
    xn = _rms(x, gmix_ref[...]).astype(_BF)
    zc = _dot(xn, wrest_ref[:, pl.ds(0, 2 * D_CONV)])
    cfull_scr[pl.ds(HIST, tm), :] = zc[:, :D_CONV] * _sigmoid(zc[:, D_CONV:])

    cb = cb_ref[...]
    lng, lnb = lng_ref[...], lnb_ref[...]
    for ts in range(tm // CONV_TS):
        t0 = ts * CONV_TS
        acc = jnp.broadcast_to(cb, (CONV_TS, D_CONV))
        for k in range(CONV_W):
            acc = acc + cw_ref[pl.ds(k, 1), :] * cfull_scr[pl.ds(t0 + k + HIST_PAD, CONV_TS), :]
        cact_scr[pl.ds(t0, CONV_TS), :] = _layernorm_silu(acc, lng, lnb).astype(_BF)

    tail = cfull_scr[pl.ds(tm, HIST), :]
    cfull_scr[pl.ds(0, HIST), :] = tail
    hist_ref[...] = tail

    o_ref[...] = _merge_and_ffn(x, xn, cact_scr[...], gy_ref[...], wrest_ref, wglu_ref, wpw_ref,
                                wout_ref, gffn_ref, wffi_ref, wffo_ref, gfin_ref, final)


def _tok_weight_specs(l, grid_rank):
    zeros = (0,) * grid_rank

    def const(shape):
        idx = (l,) + (0,) * len(shape)
        return pl.BlockSpec((None,) + shape, lambda *_: idx, pipeline_mode=pl.Buffered(1))

    return dict(
        gmix=const((1, D_MODEL)), wrest=const((D_MODEL, 2 * D_CONV + 2 * D_MODEL)),
        wglu=const((D_SSM, 2 * D_MODEL)), cw=const((CONV_W, D_CONV)), cb=const((1, D_CONV)),
        lng=const((1, D_CONV)), lnb=const((1, D_CONV)), wpw=const((D_CONV, D_MODEL)),
        wout=const((D_MODEL, D_MODEL)), gffn=const((1, D_MODEL)), wffi=const((D_MODEL, 2 * D_FF)),
        wffo=const((D_FF, D_MODEL)),
        gfin=pl.BlockSpec((1, D_MODEL), lambda *_: (0, 0), pipeline_mode=pl.Buffered(1)))


def _tok_prompt(x, gy, l, w, final):
    nb, seq, _ = x.shape
    tm = TOK_TM
    ws = _tok_weight_specs(l, 2)
    tile = lambda width: pl.BlockSpec((None, tm, width), lambda b, t: (b, t, 0))
    return pl.pallas_call(
        functools.partial(_tok_prompt_kernel, tm=tm, final=final),
        grid=(nb, seq // tm),
        in_specs=[tile(D_MODEL), tile(D_SSM), ws['gmix'], ws['wrest'], ws['wglu'], ws['cw'], ws['cb'],
                  ws['lng'], ws['lnb'], ws['wpw'], ws['wout'], ws['gffn'], ws['wffi'], ws['wffo'], ws['gfin']],
        out_specs=[tile(D_MODEL), pl.BlockSpec((None, HIST, D_CONV), lambda b, t: (b, 0, 0))],
        out_shape=[jax.ShapeDtypeStruct((nb, seq, D_MODEL), _F32),
                   jax.ShapeDtypeStruct((nb, HIST, D_CONV), _F32)],
        scratch_shapes=[pltpu.VMEM((HIST + tm, D_CONV), _F32), pltpu.VMEM((tm, D_CONV), _BF)],
        compiler_params=pltpu.CompilerParams(
            dimension_semantics=("arbitrary", "arbitrary"), vmem_limit_bytes=VMEM_LIMIT),
        name=f"tok_prompt_l{l}",
    )(x, gy, w['g_mix'], w['w_rest'], w['w_glu'], w['conv_w'], w['conv_b'], w['ln_g'], w['ln_b'],
      w['w_pw'], w['w_out'], w['g_ffn'], w['w_ff_in'], w['w_ff_out'], w['g_final'])


def _tok_sample_kernel(x_ref, gy_ref, buf_ref, gmix_ref, wrest_ref, wglu_ref, cw_ref, cb_ref, lng_ref,
                       lnb_ref, wpw_ref, wout_ref, gffn_ref, wffi_ref, wffo_ref, gfin_ref,
                       o_ref, nbuf_ref, *, final):
    x = x_ref[...]
    xn = _rms(x, gmix_ref[...]).astype(_BF)
    zc = _dot(xn, wrest_ref[:, pl.ds(0, 2 * D_CONV)])
    c = zc[:, :D_CONV] * _sigmoid(zc[:, D_CONV:])
    acc = cb_ref[...] + cw_ref[pl.ds(CONV_W - 1, 1), :] * c
    for k in range(CONV_W - 1):
        acc = acc + cw_ref[pl.ds(k, 1), :] * buf_ref[k]
    for k in range(CONV_W - 2):
        nbuf_ref[k] = buf_ref[k + 1]
    nbuf_ref[CONV_W - 2] = c
    cact = _layernorm_silu(acc, lng_ref[...], lnb_ref[...]).astype(_BF)
    o_ref[...] = _merge_and_ffn(x, xn, cact, gy_ref[...], wrest_ref, wglu_ref, wpw_ref,
                                wout_ref, gffn_ref, wffi_ref, wffo_ref, gfin_ref, final)


def _tok_sample(x, gy, buf_t, l, w, final):
    nb = x.shape[0]
    ws = _tok_weight_specs(l, 1)
    full = lambda shape: pl.BlockSpec(shape, lambda i: (0,) * len(shape))
    return pl.pallas_call(
        functools.partial(_tok_sample_kernel, final=final),
        grid=(1,),
        in_specs=[full((nb, D_MODEL)), full((nb, D_SSM)),
                  pl.BlockSpec((None, CONV_W - 1, nb, D_CONV), lambda i: (l, 0, 0, 0)),
                  ws['gmix'], ws['wrest'], ws['wglu'], ws['cw'], ws['cb'],
                  ws['lng'], ws['lnb'], ws['wpw'], ws['wout'], ws['gffn'], ws['wffi'], ws['wffo'], ws['gfin']],
        out_specs=[full((nb, D_MODEL)), full((CONV_W - 1, nb, D_CONV))],
        out_shape=[jax.ShapeDtypeStruct((nb, D_MODEL), _F32),
                   jax.ShapeDtypeStruct((CONV_W - 1, nb, D_CONV), _F32)],
        compiler_params=pltpu.CompilerParams(
            dimension_semantics=("arbitrary",), vmem_limit_bytes=VMEM_LIMIT),
        name=f"tok_sample_l{l}",
    )(x, gy, buf_t, w['g_mix'], w['w_rest'], w['w_glu'], w['conv_w'], w['conv_b'], w['ln_g'], w['ln_b'],
      w['w_pw'], w['w_out'], w['g_ffn'], w['w_ff_in'], w['w_ff_out'], w['g_final'])


def kernel(x_prompt, x_sample, state_ssm_re, state_ssm_im, state_conv, g_mix, w_in, ssm_a_re, ssm_a_im,
           ssm_log_dt, ssm_b_re, ssm_b_im, ssm_c_re, ssm_c_im, ssm_d, w_glu, conv_w, conv_b, conv_ln_g,
           conv_ln_b, w_pw, w_out, g_ffn, w_ff_in, w_ff_out, g_final):
    nbp = x_prompt.shape[0]
    nbs = x_sample.shape[0]
    row = lambda a: a.astype(_F32).reshape(DEPTH, 1, -1)
    s5 = _s5_params(ssm_a_re, ssm_a_im, ssm_log_dt, ssm_b_re, ssm_b_im, ssm_c_re, ssm_c_im)
    w = dict(
        g_mix=row(g_mix), w_rest=w_in[:, :, D_SSM:].astype(_BF), w_glu=w_glu.astype(_BF),
        conv_w=conv_w.astype(_F32), conv_b=row(conv_b), ln_g=row(conv_ln_g), ln_b=row(conv_ln_b),
        w_pw=w_pw.astype(_BF), w_out=w_out.astype(_BF), g_ffn=row(g_ffn),
        w_ff_in=w_ff_in.astype(_BF), w_ff_out=w_ff_out.astype(_BF),
        g_final=g_final.astype(_F32).reshape(1, D_MODEL))
    w_u = w_in[:, :, :D_SSM].astype(_BF)
    d_row = row(ssm_d)

    xp = x_prompt
    xs = x_sample.reshape(nbs, D_MODEL)
    h_re = state_ssm_re.reshape(DEPTH, nbs, N_GROUPS * P_STATE)
    h_im = state_ssm_im.reshape(DEPTH, nbs, N_GROUPS * P_STATE)
    buf_t = state_conv.transpose(0, 2, 1, 3)

    re_p, im_p, conv_p, re_s, im_s, conv_s = [], [], [], [], [], []
    for l in range(DEPTH):
        final = l == DEPTH - 1
        gy, hfin = _seq_prompt(xp, l, w['g_mix'], w_u, s5, d_row)
        xp, hist = _tok_prompt(xp, gy, l, w, final)
        hfin = hfin.reshape(nbp, N_GROUPS, 2, P_STATE)
        re_p.append(hfin[:, :, 0])
        im_p.append(hfin[:, :, 1])
        conv_p.append(hist[:, HIST_PAD:])

        gys, nre, nim = _seq_sample(xs, h_re, h_im, l, w['g_mix'], w_u, s5, d_row)
        xs, nbuf = _tok_sample(xs, gys, buf_t, l, w, final)
        re_s.append(nre.reshape(nbs, N_GROUPS, P_STATE))
        im_s.append(nim.reshape(nbs, N_GROUPS, P_STATE))
        conv_s.append(nbuf.transpose(1, 0, 2))

    return (xp, xs.reshape(nbs, 1, D_MODEL), jnp.stack(re_p), jnp.stack(im_p), jnp.stack(conv_p),
            jnp.stack(re_s), jnp.stack(im_s), jnp.stack(conv_s))
```

```python
import functools

import numpy as np
import jax
import jax.numpy as jnp
from jax import lax
from jax.experimental import pallas as pl
from jax.experimental.pallas import tpu as pltpu

D_MODEL = 1024
DEPTH = 4
D_SSM = 512
GROUP = 16
N_GROUPS = 32
P_STATE = 64
D_CONV = 512
CONV_W = 31
D_FF = 2816
EPS = 1e-6

LANES = 128
SUBLANES = 8
MXU = 256
LC = 16
SLOTS = LANES // GROUP
STATE_W = 2 * P_STATE
N_STATE = N_GROUPS * P_STATE
HIST = 32
HIST_PAD = HIST - (CONV_W - 1)
FF_CHUNK = MXU
VMEM_LIMIT = 56 * 1024 * 1024

SEQ_TT = 128
TOK_TM = 256
CONV_TS = 64

_BF = jnp.bfloat16
_F32 = jnp.float32


def _dot(a, b):
    return jnp.dot(a, b, preferred_element_type=_F32)


def _rms(xf, g):
    return xf * lax.rsqrt(jnp.mean(xf * xf, axis=-1, keepdims=True) + EPS) * g


def _sigmoid(x):
    return jax.nn.sigmoid(x)


def _layernorm_silu(c, g, b):
    mu = jnp.mean(c, axis=-1, keepdims=True)
    var = jnp.mean(jnp.square(c - mu), axis=-1, keepdims=True)
    y = (c - mu) * lax.rsqrt(var + EPS) * g + b
    return y * _sigmoid(y)


def _slot_times():
    g8 = np.arange(N_GROUPS)[:, None, None] % SLOTS
    half = np.arange(LC // SLOTS)[None, :, None]
    slot = np.arange(SLOTS)[None, None, :]
    j_in = half * SLOTS + (slot - g8) % SLOTS
    j_out = half * SLOTS + (g8 - slot) % SLOTS
    return j_in, j_out


def _s5_params(a_re, a_im, log_dt, b_re, b_im, c_re, c_im):
    hi = lax.Precision.HIGHEST
    a_re, a_im = a_re.astype(_F32), a_im.astype(_F32)
    dt = jnp.exp(log_dt.astype(_F32))[..., None]
    lre, lim = dt * a_re, dt * a_im
    n = jnp.arange(LC + 1, dtype=_F32)[None, None, :, None]
    pmag = jnp.exp(n * lre[:, :, None, :])
    pw_re = pmag * jnp.cos(n * lim[:, :, None, :])
    pw_im = pmag * jnp.sin(n * lim[:, :, None, :])
    ab_re, ab_im = pw_re[:, :, 1], pw_im[:, :, 1]
    nr, ni = ab_re - 1.0, ab_im
    den = a_re * a_re + a_im * a_im
    q_re = (nr * a_re + ni * a_im) / den
    q_im = (ni * a_re - nr * a_im) / den
    b_re, b_im = b_re.astype(_F32), b_im.astype(_F32)
    bb_re = q_re[..., None] * b_re - q_im[..., None] * b_im
    bb_im = q_re[..., None] * b_im + q_im[..., None] * b_re
    c_re, c_im = c_re.astype(_F32), c_im.astype(_F32)
    ca_re = c_re[:, :, None] * pw_re[:, :, :, None, :] - c_im[:, :, None] * pw_im[:, :, :, None, :]
    ca_im = c_re[:, :, None] * pw_im[:, :, :, None, :] + c_im[:, :, None] * pw_re[:, :, :, None, :]
    kern = (jnp.einsum('lgdcp,lgpe->lgdce', ca_re[:, :, :LC], bb_re, precision=hi)
            - jnp.einsum('lgdcp,lgpe->lgdce', ca_im[:, :, :LC], bb_im, precision=hi))

    j_in, j_out = _slot_times()
    j_in_f = j_in.reshape(N_GROUPS, LC)
    j_out_f = j_out.reshape(N_GROUPS, LC)
    gi = np.arange(N_GROUPS)

    d = j_out_f[:, None, :] - j_in_f[:, :, None]
    sel = (d[..., None] == np.arange(LC)).astype(np.float32)
    toe = jnp.einsum('gabd,lgdce->lgaebc', jnp.asarray(sel, _BF), kern.astype(_BF),
                     preferred_element_type=_BF)
    toe = toe.reshape(DEPTH, N_GROUPS, MXU, MXU)

    pin_re = pw_re[:, gi[:, None, None], LC - 1 - j_in]
    pin_im = pw_im[:, gi[:, None, None], LC - 1 - j_in]
    bt_re = bb_re.transpose(0, 1, 3, 2)[:, :, None, None]
    bt_im = bb_im.transpose(0, 1, 3, 2)[:, :, None, None]
    win_re = pin_re[:, :, :, :, None, :] * bt_re - pin_im[:, :, :, :, None, :] * bt_im
    win_im = pin_re[:, :, :, :, None, :] * bt_im + pin_im[:, :, :, :, None, :] * bt_re
    w_sin = jnp.stack([win_re, win_im], axis=-2).astype(_BF).reshape(DEPTH, N_GROUPS, MXU, STATE_W)

    co_re = ca_re[:, gi[:, None, None], j_out + 1]
    co_im = ca_im[:, gi[:, None, None], j_out + 1]
    w_sout = jnp.stack([co_re, -co_im], axis=2).astype(_BF)
    w_sout = w_sout.transpose(0, 1, 2, 6, 3, 4, 5).reshape(DEPTH, N_GROUPS, STATE_W, MXU)

    def mult_rows(p_re, p_im):
        m1 = jnp.concatenate([p_re, p_re], axis=-1).reshape(DEPTH, 1, N_GROUPS * STATE_W)
        m2 = jnp.concatenate([-p_im, p_im], axis=-1).reshape(DEPTH, 1, N_GROUPS * STATE_W)
        return jnp.concatenate([m1, m2], axis=1)

    return dict(toe=toe, w_sin=w_sin, w_sout=w_sout,
                a_lc=mult_rows(pw_re[:, :, LC], pw_im[:, :, LC]), a_1=mult_rows(ab_re, ab_im))


def _seq_prompt_kernel(x_ref, gmix_ref, wu_ref, toe_ref, wsin_ref, wsout_ref, alc_ref, d_ref,
                       gy_ref, hfin_ref,
                       u_scr, lhs_scr, e_scr, hs_scr, yg_scr, y_scr, hc_scr, *, tt):
    nb = x_ref.shape[0]
    rows = nb * tt
    m = rows // LC
    nk = tt // LC

    @pl.when(pl.program_id(0) == 0)
    def _():
        hc_scr[...] = jnp.zeros_like(hc_scr)

    x = x_ref[...].reshape(rows, D_MODEL)
    xn = _rms(x, gmix_ref[...]).astype(_BF)
    u = _dot(xn, wu_ref[...])
    for v in range(D_SSM // LANES):
        u_scr[v] = u[:, LANES * v:LANES * (v + 1)]

    slot = lax.broadcasted_iota(jnp.int32, (m, LANES), 1) // GROUP

    for v in range(D_SSM // LANES):
        rolled = []
        for j in range(LC):
            piece = u_scr[v, pl.ds(j, m, stride=LC), :]
            r = j % SLOTS
            rolled.append(pltpu.roll(piece, GROUP * r, 1) if r else piece)
        for gl in range(SLOTS):
            halves = []
            for h in range(LC // SLOTS):
                acc = rolled[SLOTS * h]
                for jl in range(1, SLOTS):
                    acc = jnp.where(slot == (gl + jl) % SLOTS, rolled[SLOTS * h + jl], acc)
                halves.append(acc)
            lhs_scr[SLOTS * v + gl] = jnp.concatenate(halves, axis=1).astype(_BF)

    for g in range(N_GROUPS):
        e_scr[g] = _dot(lhs_scr[g], wsin_ref[g])

    for g in range(N_GROUPS):
        sl = pl.ds(STATE_W * g, STATE_W)
        m1 = alc_ref[0:1, sl]
        m2 = alc_ref[1:2, sl]
        h = hc_scr[:, sl]
        for k in range(nk):
            hs_scr[g, pl.ds(k, nb, stride=nk), :] = h
            e = e_scr[g, pl.ds(k, nb, stride=nk), :]
            h = h * m1 + pltpu.roll(h, P_STATE, 1) * m2 + e
        hc_scr[:, sl] = h
        hfin_ref[:, sl] = h

    for g in range(N_GROUPS):
        yg_scr[g] = _dot(lhs_scr[g], toe_ref[g]) + _dot(hs_scr[g].astype(_BF), wsout_ref[g])

    for v in range(D_SSM // LANES):
        dv = d_ref[:, pl.ds(LANES * v, LANES)]
        for j in range(LC):
            hf, jl = divmod(j, SLOTS)
            acc = yg_scr[SLOTS * v, :, pl.ds(LANES * hf, LANES)]
            for gl in range(1, SLOTS):
                acc = jnp.where(slot == (gl - jl) % SLOTS,
                                yg_scr[SLOTS * v + gl, :, pl.ds(LANES * hf, LANES)], acc)
            if jl:
                acc = pltpu.roll(acc, GROUP * jl, 1)
            y_scr[v, pl.ds(j, m, stride=LC), :] = acc + dv * u_scr[v, pl.ds(j, m, stride=LC), :]

    y = jnp.concatenate([y_scr[v] for v in range(D_SSM // LANES)], axis=1)
    gy_ref[...] = jax.nn.gelu(y).astype(_BF).reshape(nb, tt, D_SSM)


def _layer_const(l, shape, index=None, **kw):
    idx = (l,) + (tuple(index) if index is not None else (0,) * len(shape))
    return pl.BlockSpec((None,) + tuple(shape), lambda *_: idx, **kw)


def _seq_prompt(x, l, gmix, w_uc, s5, ssm_d):
    nb, seq, _ = x.shape
    tt = SEQ_TT
    rows = nb * tt
    m = rows // LC
    return pl.pallas_call(
        functools.partial(_seq_prompt_kernel, tt=tt),
        grid=(seq // tt,),
        in_specs=[
            pl.BlockSpec((nb, tt, D_MODEL), lambda i: (0, i, 0)),
            _layer_const(l, (1, D_MODEL)),
            _layer_const(l, (D_MODEL, D_SSM)),
            _layer_const(l, (N_GROUPS, MXU, MXU)),
            _layer_const(l, (N_GROUPS, MXU, STATE_W)),
            _layer_const(l, (N_GROUPS, STATE_W, MXU)),
            _layer_const(l, (2, N_GROUPS * STATE_W)),
            _layer_const(l, (1, D_SSM)),
        ],
        out_specs=[
            pl.BlockSpec((nb, tt, D_SSM), lambda i: (0, i, 0)),
            pl.BlockSpec((nb, N_GROUPS * STATE_W), lambda i: (0, 0)),
        ],
        out_shape=[
            jax.ShapeDtypeStruct((nb, seq, D_SSM), _BF),
            jax.ShapeDtypeStruct((nb, N_GROUPS * STATE_W), _F32),
        ],
        scratch_shapes=[
            pltpu.VMEM((D_SSM // LANES, rows, LANES), _F32),
            pltpu.VMEM((N_GROUPS, m, MXU), _BF),
            pltpu.VMEM((N_GROUPS, m, STATE_W), _F32),
            pltpu.VMEM((N_GROUPS, m, STATE_W), _F32),
            pltpu.VMEM((N_GROUPS, m, MXU), _F32),
            pltpu.VMEM((D_SSM // LANES, rows, LANES), _F32),
            pltpu.VMEM((nb, N_GROUPS * STATE_W), _F32),
        ],
        compiler_params=pltpu.CompilerParams(
            dimension_semantics=("arbitrary",), vmem_limit_bytes=VMEM_LIMIT),
        name=f"seq_prompt_l{l}",
    )(x, gmix, w_uc, s5['toe'], s5['w_sin'], s5['w_sout'], s5['a_lc'], ssm_d)


def _seq_sample_kernel(x_ref, hre_ref, him_ref, buf_ref, gmix_ref, wuc_ref, toe_ref, wsin_ref, wsout_ref,
                       a1_ref, d_ref, cw_ref, cb_ref, lng_ref, lnb_ref,
                       gy_ref, cact_ref, nre_ref, nim_ref, nbuf_ref):
    nb = x_ref.shape[0]
    xn = _rms(x_ref[...], gmix_ref[...]).astype(_BF)
    z = _dot(xn, wuc_ref[...])
    u = z[:, :D_SSM]
    c = z[:, D_SSM:D_SSM + D_CONV] * _sigmoid(z[:, D_SSM + D_CONV:])

    lane = lax.broadcasted_iota(jnp.int32, (nb, LANES), 1)
    slot = lane // GROUP
    lo = lane < P_STATE
    zero = jnp.zeros((nb, LANES), _F32)
    last = LC - 1

    ys = []
    for v in range(D_SSM // LANES):
        uv = u[:, LANES * v:LANES * (v + 1)]
        u_last = pltpu.roll(uv, GROUP * (last % SLOTS), 1)
        yv = None
        for gp in range(SLOTS // 2):
            sv = (SLOTS * v) // 2 + gp
            rv = hre_ref[:, pl.ds(LANES * sv, LANES)]
            iv = him_ref[:, pl.ds(LANES * sv, LANES)]
            h0s = (jnp.where(lo, rv, pltpu.roll(iv, P_STATE, 1)),
                   jnp.where(lo, pltpu.roll(rv, P_STATE, 1), iv))
            hn = []
            for par in range(2):
                gl = 2 * gp + par
                g = SLOTS * v + gl
                h0 = h0s[par]
                lhs_l = jnp.concatenate(
                    [zero, jnp.where(slot == (gl + last) % SLOTS, u_last, 0.0)], axis=1).astype(_BF)
                e = _dot(lhs_l, wsin_ref[g])
                sl = pl.ds(STATE_W * g, STATE_W)
                hn.append(h0 * a1_ref[0:1, sl] + pltpu.roll(h0, P_STATE, 1) * a1_ref[1:2, sl] + e)
                lhs_f = jnp.concatenate([jnp.where(slot == gl, uv, 0.0), zero], axis=1).astype(_BF)
                yg = (_dot(lhs_f, toe_ref[g, :, pl.ds(0, LANES)])
                      + _dot(h0.astype(_BF), wsout_ref[g, :, pl.ds(0, LANES)]))
                yv = yg if yv is None else jnp.where(slot == gl, yg, yv)
            nre_ref[:, pl.ds(LANES * sv, LANES)] = jnp.where(lo, hn[0], pltpu.roll(hn[1], P_STATE, 1))
            nim_ref[:, pl.ds(LANES * sv, LANES)] = jnp.where(lo, pltpu.roll(hn[0], P_STATE, 1), hn[1])
        ys.append(yv + d_ref[:, pl.ds(LANES * v, LANES)] * uv)
    gy_ref[...] = jax.nn.gelu(jnp.concatenate(ys, axis=1)).astype(_BF)

    nlb = D_CONV // LANES
    per_b = (CONV_W - 1) * nlb
    tap_rows = lambda k, lb: pl.ds(nlb * k + lb, nb, stride=per_b)
    accs = []
    for lb in range(nlb):
        ls = pl.ds(LANES * lb, LANES)
        cl = c[:, LANES * lb:LANES * (lb + 1)]
        acc = cb_ref[:, ls] + cw_ref[pl.ds(CONV_W - 1, 1), ls] * cl
        for k in range(CONV_W - 1):
            tap = buf_ref[tap_rows(k, lb), :]
            acc = acc + cw_ref[pl.ds(k, 1), ls] * tap
            if k:
                nbuf_ref[tap_rows(k - 1, lb), :] = tap
        nbuf_ref[tap_rows(CONV_W - 2, lb), :] = cl
        accs.append(acc)
    cact_ref[...] = _layernorm_silu(jnp.concatenate(accs, axis=1), lng_ref[...], lnb_ref[...]).astype(_BF)


def _seq_sample(x, h_re, h_im, buf, l, w, w_uc, s5):
    nb = x.shape[0]
    flat = nb * (CONV_W - 1) * (D_CONV // LANES)
    one = pl.Buffered(1)
    full = lambda shape: pl.BlockSpec(shape, lambda i: (0,) * len(shape))
    lc = functools.partial(_layer_const, l, pipeline_mode=one)
    return pl.pallas_call(
        _seq_sample_kernel,
        grid=(1,),
        in_specs=[
            full((nb, D_MODEL)), lc((nb, N_STATE)), lc((nb, N_STATE)), lc((flat, LANES)),
            lc((1, D_MODEL)), lc((D_MODEL, D_SSM + 2 * D_CONV)),
            lc((N_GROUPS, MXU, MXU)), lc((N_GROUPS, MXU, STATE_W)), lc((N_GROUPS, STATE_W, MXU)),
            lc((2, N_GROUPS * STATE_W)), lc((1, D_SSM)),
            lc((CONV_W, D_CONV)), lc((1, D_CONV)), lc((1, D_CONV)), lc((1, D_CONV)),
        ],
        out_specs=[full((nb, D_SSM)), full((nb, D_CONV)), full((nb, N_STATE)), full((nb, N_STATE)),
                   full((flat, LANES))],
        out_shape=[
            jax.ShapeDtypeStruct((nb, D_SSM), _BF),
            jax.ShapeDtypeStruct((nb, D_CONV), _BF),
            jax.ShapeDtypeStruct((nb, N_STATE), _F32),
            jax.ShapeDtypeStruct((nb, N_STATE), _F32),
            jax.ShapeDtypeStruct((flat, LANES), _F32),
        ],
        compiler_params=pltpu.CompilerParams(
            dimension_semantics=("arbitrary",), vmem_limit_bytes=VMEM_LIMIT),
        name=f"seq_sample_l{l}",
    )(x, h_re, h_im, buf, w['g_mix'], w_uc, s5['toe'], s5['w_sin'], s5['w_sout'], s5['a_1'], w['ssm_d'],
      w['conv_w'], w['conv_b'], w['ln_g'], w['ln_b'])


def _merge_and_ffn(x, xn, cact, gy, wgate_ref, wglu_ref, wpw_ref, wout_ref, gffn_ref,
                   wffi_ref, wffo_ref, gfin_ref, final):
    yg = _dot(gy, wglu_ref[...])
    ya = yg[:, :D_MODEL] * _sigmoid(yg[:, D_MODEL:])
    yb = _dot(cact, wpw_ref[...])
    ga = _sigmoid(_dot(xn, wgate_ref[:, pl.ds(0, D_MODEL)]))
    gb = _sigmoid(_dot(xn, wgate_ref[:, pl.ds(D_MODEL, D_MODEL)]))
    merged = (ga * ya + gb * yb).astype(_BF)
    x1 = x + _dot(merged, wout_ref[...])
    hn = _rms(x1, gffn_ref[...]).astype(_BF)
    acc = x1
    for ck in range(D_FF // FF_CHUNK):
        h1 = _dot(hn, wffi_ref[:, pl.ds(FF_CHUNK * ck, FF_CHUNK)])
        h2 = _dot(hn, wffi_ref[:, pl.ds(D_FF + FF_CHUNK * ck, FF_CHUNK)])
        a = (h1 * _sigmoid(h1) * h2).astype(_BF)
        acc = acc + _dot(a, wffo_ref[pl.ds(FF_CHUNK * ck, FF_CHUNK), :])
    if final:
        acc = _rms(acc, gfin_ref[...])
    return acc


def _tok_prompt_kernel(x_ref, gy_ref, gmix_ref, wcv_ref, wcg_ref, cw_ref, cb_ref, lng_ref, lnb_ref,
                       wgate_ref, wglu_ref, wpw_ref, wout_ref, gffn_ref, wffi_ref, wffo_ref, gfin_ref,
                       o_ref, hist_ref, cfull_scr, shift_scr, conv_scr, *, tm, final):
    @pl.when(pl.program_id(1) == 0)
    def _():
        cfull_scr[pl.ds(0, HIST), :] = jnp.zeros((HIST, D_CONV), _F32)

    x = x_ref[...]
    xn = _rms(x, gmix_ref[...]).astype(_BF)
    cfull_scr[pl.ds(HIST, tm), :] = _dot(xn, wcv_ref[...]) * _sigmoid(_dot(xn, wcg_ref[...]))

    for r in range(SUBLANES):
        span = tm + SUBLANES * ((CONV_W - 1 - r) // SUBLANES)
        shift_scr[r, pl.ds(0, span), :] = cfull_scr[pl.ds(r + HIST_PAD, span), :]
    for ts in range(tm // CONV_TS):
        t0 = ts * CONV_TS
        for lb in range(D_CONV // LANES):
            ls = pl.ds(LANES * lb, LANES)
            acc = jnp.broadcast_to(cb_ref[:, ls], (CONV_TS, LANES))
            for k in range(CONV_W):
                r = k % SUBLANES
                acc = acc + cw_ref[pl.ds(k, 1), ls] * shift_scr[r, pl.ds(t0 + k - r, CONV_TS), ls]
            conv_scr[pl.ds(t0, CONV_TS), ls] = acc
    cact = _layernorm_silu(conv_scr[...], lng_ref[...], lnb_ref[...]).astype(_BF)

    tail = cfull_scr[pl.ds(tm, HIST), :]
    cfull_scr[pl.ds(0, HIST), :] = tail
    hist_ref[...] = tail

    o_ref[...] = _merge_and_ffn(x, xn, cact, gy_ref[...], wgate_ref, wglu_ref, wpw_ref,
                                wout_ref, gffn_ref, wffi_ref, wffo_ref, gfin_ref, final)


def _tok_weight_specs(l):
    lc = functools.partial(_layer_const, l, pipeline_mode=pl.Buffered(1))
    return dict(
        gmix=lc((1, D_MODEL)), wgate=lc((D_MODEL, 2 * D_MODEL)),
        wglu=lc((D_SSM, 2 * D_MODEL)), wpw=lc((D_CONV, D_MODEL)),
        wout=lc((D_MODEL, D_MODEL)), gffn=lc((1, D_MODEL)), wffi=lc((D_MODEL, 2 * D_FF)),
        wffo=lc((D_FF, D_MODEL)),
        gfin=pl.BlockSpec((1, D_MODEL), lambda *_: (0, 0), pipeline_mode=pl.Buffered(1)))


def _tok_prompt(x, gy, l, w, w_uc, final):
    nb, seq, _ = x.shape
    tm = TOK_TM
    ws = _tok_weight_specs(l)
    lc = functools.partial(_layer_const, l, pipeline_mode=pl.Buffered(1))
    tile = lambda width: pl.BlockSpec((None, tm, width), lambda b, t: (b, t, 0))
    return pl.pallas_call(
        functools.partial(_tok_prompt_kernel, tm=tm, final=final),
        grid=(nb, seq // tm),
        in_specs=[tile(D_MODEL), tile(D_SSM), ws['gmix'],
                  lc((D_MODEL, D_CONV), index=(0, D_SSM // D_CONV)),
                  lc((D_MODEL, D_CONV), index=(0, D_SSM // D_CONV + 1)),
                  lc((CONV_W, D_CONV)), lc((1, D_CONV)), lc((1, D_CONV)), lc((1, D_CONV)),
                  ws['wgate'], ws['wglu'], ws['wpw'], ws['wout'], ws['gffn'], ws['wffi'], ws['wffo'],
                  ws['gfin']],
        out_specs=[tile(D_MODEL), pl.BlockSpec((None, HIST, D_CONV), lambda b, t: (b, 0, 0))],
        out_shape=[jax.ShapeDtypeStruct((nb, seq, D_MODEL), _F32),
                   jax.ShapeDtypeStruct((nb, HIST, D_CONV), _F32)],
        scratch_shapes=[pltpu.VMEM((HIST + tm, D_CONV), _F32),
                        pltpu.VMEM((SUBLANES, HIST + tm, D_CONV), _F32),
                        pltpu.VMEM((tm, D_CONV), _F32)],
        compiler_params=pltpu.CompilerParams(
            dimension_semantics=("arbitrary", "arbitrary"), vmem_limit_bytes=VMEM_LIMIT),
        name=f"tok_prompt_l{l}",
    )(x, gy, w['g_mix'], w_uc, w_uc, w['conv_w'], w['conv_b'], w['ln_g'], w['ln_b'],
      w['w_gate'], w['w_glu'], w['w_pw'], w['w_out'], w['g_ffn'], w['w_ff_in'], w['w_ff_out'], w['g_final'])


def _tok_sample_kernel(x_ref, gy_ref, cact_ref, gmix_ref, wgate_ref, wglu_ref, wpw_ref, wout_ref,
                       gffn_ref, wffi_ref, wffo_ref, gfin_ref, o_ref, *, final):
    x = x_ref[...]
    xn = _rms(x, gmix_ref[...]).astype(_BF)
    o_ref[...] = _merge_and_ffn(x, xn, cact_ref[...], gy_ref[...], wgate_ref, wglu_ref, wpw_ref,
                                wout_ref, gffn_ref, wffi_ref, wffo_ref, gfin_ref, final)


def _tok_sample(x, gy, cact, l, w, final):
    nb = x.shape[0]
    ws = _tok_weight_specs(l)
    full = lambda shape: pl.BlockSpec(shape, lambda i: (0,) * len(shape))
    return pl.pallas_call(
        functools.partial(_tok_sample_kernel, final=final),
        grid=(1,),
        in_specs=[full((nb, D_MODEL)), full((nb, D_SSM)), full((nb, D_CONV)),
                  ws['gmix'], ws['wgate'], ws['wglu'], ws['wpw'], ws['wout'], ws['gffn'], ws['wffi'],
                  ws['wffo'], ws['gfin']],
        out_specs=full((nb, D_MODEL)),
        out_shape=jax.ShapeDtypeStruct((nb, D_MODEL), _F32),
        compiler_params=pltpu.CompilerParams(
            dimension_semantics=("arbitrary",), vmem_limit_bytes=VMEM_LIMIT),
        name=f"tok_sample_l{l}",
    )(x, gy, cact, w['g_mix'], w['w_gate'], w['w_glu'], w['w_pw'], w['w_out'], w['g_ffn'],
      w['w_ff_in'], w['w_ff_out'], w['g_final'])


def kernel(x_prompt, x_sample, state_ssm_re, state_ssm_im, state_conv, g_mix, w_in, ssm_a_re, ssm_a_im,
           ssm_log_dt, ssm_b_re, ssm_b_im, ssm_c_re, ssm_c_im, ssm_d, w_glu, conv_w, conv_b, conv_ln_g,
           conv_ln_b, w_pw, w_out, g_ffn, w_ff_in, w_ff_out, g_final):
    nbp = x_prompt.shape[0]
    nbs = x_sample.shape[0]
    row = lambda a: a.astype(_F32).reshape(DEPTH, 1, -1)
    s5 = _s5_params(ssm_a_re, ssm_a_im, ssm_log_dt, ssm_b_re, ssm_b_im, ssm_c_re, ssm_c_im)
    n_uc = D_SSM + 2 * D_CONV
    w_uc = w_in[:, :, :n_uc].astype(_BF)
    w = dict(
        g_mix=row(g_mix), w_gate=w_in[:, :, n_uc:].astype(_BF), w_glu=w_glu.astype(_BF),
        conv_w=conv_w.astype(_F32), conv_b=row(conv_b), ln_g=row(conv_ln_g), ln_b=row(conv_ln_b),
        w_pw=w_pw.astype(_BF), w_out=w_out.astype(_BF), g_ffn=row(g_ffn),
        w_ff_in=w_ff_in.astype(_BF), w_ff_out=w_ff_out.astype(_BF),
        g_final=g_final.astype(_F32).reshape(1, D_MODEL), ssm_d=row(ssm_d))

    xp = x_prompt
    xs = x_sample.reshape(nbs, D_MODEL)
    h_re = state_ssm_re.reshape(DEPTH, nbs, N_STATE)
    h_im = state_ssm_im.reshape(DEPTH, nbs, N_STATE)
    buf = state_conv.reshape(DEPTH, -1, LANES)

    re_p, im_p, conv_p, re_s, im_s, conv_s = [], [], [], [], [], []
    for l in range(DEPTH):
        final = l == DEPTH - 1
        gy, hfin = _seq_prompt(xp, l, w['g_mix'], w_uc, s5, w['ssm_d'])
        xp, hist = _tok_prompt(xp, gy, l, w, w_uc, final)
        hfin = hfin.reshape(nbp, N_GROUPS, 2, P_STATE)
        re_p.append(hfin[:, :, 0])
        im_p.append(hfin[:, :, 1])
        conv_p.append(hist[:, HIST_PAD:])

        gys, cacts, nre, nim, nbuf = _seq_sample(xs, h_re, h_im, buf, l, w, w_uc, s5)
        xs = _tok_sample(xs, gys, cacts, l, w, final)
        re_s.append(nre.reshape(nbs, N_GROUPS, P_STATE))
        im_s.append(nim.reshape(nbs, N_GROUPS, P_STATE))
        conv_s.append(nbuf.reshape(nbs, CONV_W - 1, D_CONV))

    return (xp, xs.reshape(nbs, 1, D_MODEL), jnp.stack(re_p), jnp.stack(im_p), jnp.stack(conv_p),
            jnp.stack(re_s), jnp.stack(im_s), jnp.stack(conv_s))
```

```python
import functools

import numpy as np
import jax
import jax.numpy as jnp
from jax import lax
from jax.experimental import pallas as pl
from jax.experimental.pallas import tpu as pltpu

D_MODEL = 1024
DEPTH = 4
D_SSM = 512
GROUP = 16
N_GROUPS = 32
P_STATE = 64
D_CONV = 512
CONV_W = 31
D_FF = 2816
EPS = 1e-6

LANES = 128
SUBLANES = 8
MXU = 256
LC = 16
SLOTS = LANES // GROUP
STATE_W = 2 * P_STATE
N_STATE = N_GROUPS * P_STATE
HIST = 32
HIST_PAD = HIST - (CONV_W - 1)
FF_CHUNK = MXU
VMEM_LIMIT = 56 * 1024 * 1024

SEQ_TT = 128
TOK_TM = 256
CONV_TS = 64

_BF = jnp.bfloat16
_F32 = jnp.float32


def _dot(a, b):
    return jnp.dot(a, b, preferred_element_type=_F32)


def _rms(xf, g):
    return xf * lax.rsqrt(jnp.mean(xf * xf, axis=-1, keepdims=True) + EPS) * g


def _sigmoid(x):
    return jax.nn.sigmoid(x)


def _layernorm_silu(c, g, b):
    mu = jnp.mean(c, axis=-1, keepdims=True)
    var = jnp.mean(jnp.square(c - mu), axis=-1, keepdims=True)
    y = (c - mu) * lax.rsqrt(var + EPS) * g + b
    return y * _sigmoid(y)


N_POW = LC + 1
ROW_PW1 = 0
ROW_PW2 = N_POW
ROW_Q1 = 2 * N_POW
ROW_Q2 = 2 * N_POW + 1
TAB_ROWS = 40


def _swap(v):
    return pltpu.roll(v, P_STATE, 1)


def _s5_ops_kernel(tab_ref, bt_ref, cx_ref, toe_ref, wsin_ref, wsout_ref):
    lane = lax.broadcasted_iota(jnp.int32, (GROUP, LANES), 1)
    zeros = jnp.zeros((GROUP, LANES), _F32)

    def one_group(g, g8):
        row = lambda r: tab_ref[g, pl.ds(r, 1), :]
        bt0 = bt_ref[g]
        bbar = bt0 * row(ROW_Q1) + _swap(bt0) * row(ROW_Q2)
        bbar_s = _swap(bbar)
        cx = cx_ref[g]
        cxs = _swap(cx)
        ca = [cx * row(ROW_PW1 + n) - cxs * row(ROW_PW2 + n) for n in range(N_POW)]
        kt = lax.dot_general(bbar, jnp.concatenate(ca[:LC], axis=0), (((1,), (1,)), ((), ())),
                             precision=lax.Precision.HIGHEST, preferred_element_type=_F32)
        kt0, kt1 = kt[:, :LANES], kt[:, LANES:]
        for j in range(LC):
            jh, jl = divmod(j, SLOTS)
            off = LANES * jh + GROUP * ((jl + g8) % SLOTS)
            r0 = pltpu.roll(kt0, GROUP * jl, 1) if jl else kt0
            r1 = pltpu.roll(kt1, GROUP * jl, 1) if jl else kt1
            lo = lane < GROUP * jl
            if jh == 0:
                o0 = jnp.where(lo, 0.0, r0) if jl else r0
                o1 = jnp.where(lo, r0, r1) if jl else r1
            else:
                o0 = zeros
                o1 = jnp.where(lo, 0.0, r0) if jl else r0
            if g8:
                o0 = pltpu.roll(o0, GROUP * g8, 1)
                o1 = pltpu.roll(o1, GROUP * g8, 1)
            toe_ref[g, pl.ds(off, GROUP), :] = jnp.concatenate([o0, o1], axis=1).astype(_BF)
            n = LC - 1 - j
            wsin_ref[g, pl.ds(off, GROUP), :] = (
                bbar * row(ROW_PW1 + n) + bbar_s * row(ROW_PW2 + n)).astype(_BF)
        wt = jnp.concatenate(ca[1:], axis=0).T
        w0, w1 = wt[:, :LANES], wt[:, LANES:]
        if g8:
            w0 = pltpu.roll(w0, GROUP * g8, 1)
            w1 = pltpu.roll(w1, GROUP * g8, 1)
        wsout_ref[g] = jnp.concatenate([w0, w1], axis=1).astype(_BF)

    def octet(q, carry):
        for g8 in range(SLOTS):
            one_group(q * SLOTS + g8, g8)
        return carry

    lax.fori_loop(0, N_GROUPS // SLOTS, octet, 0)


def _s5_params(a_re, a_im, log_dt, b_re, b_im, c_re, c_im):
    a_re, a_im = a_re.astype(_F32), a_im.astype(_F32)
    dt = jnp.exp(log_dt.astype(_F32))[..., None]
    lre, lim = dt * a_re, dt * a_im
    n = jnp.arange(LC + 1, dtype=_F32)[None, None, :, None]
    pmag = jnp.exp(n * lre[:, :, None, :])
    pw_re = pmag * jnp.cos(n * lim[:, :, None, :])
    pw_im = pmag * jnp.sin(n * lim[:, :, None, :])
    ab_re, ab_im = pw_re[:, :, 1], pw_im[:, :, 1]
    nr, ni = ab_re - 1.0, ab_im
    den = a_re * a_re + a_im * a_im
    q_re = (nr * a_re + ni * a_im) / den
    q_im = (ni * a_re - nr * a_im) / den
    both = lambda a, b: jnp.concatenate([a, b], axis=-1)
    pw1 = both(pw_re, pw_re)
    pw2 = both(-pw_im, pw_im)
    tab = jnp.concatenate(
        [pw1, pw2, both(q_re, q_re)[:, :, None], both(-q_im, q_im)[:, :, None],
         jnp.zeros((DEPTH, N_GROUPS, TAB_ROWS - 2 * N_POW - 2, STATE_W), _F32)], axis=2)
    bt0 = both(b_re.astype(_F32).transpose(0, 1, 3, 2), b_im.astype(_F32).transpose(0, 1, 3, 2))
    cx = both(c_re.astype(_F32), -c_im.astype(_F32))

    per_layer = lambda *shape: pl.BlockSpec((None,) + shape, lambda l: (l,) + (0,) * len(shape))
    toe, w_sin, w_sout = pl.pallas_call(
        _s5_ops_kernel,
        grid=(DEPTH,),
        in_specs=[per_layer(N_GROUPS, TAB_ROWS, STATE_W), per_layer(N_GROUPS, GROUP, STATE_W),
                  per_layer(N_GROUPS, GROUP, STATE_W)],
        out_specs=[per_layer(N_GROUPS, MXU, MXU), per_layer(N_GROUPS, MXU, STATE_W),
                   per_layer(N_GROUPS, STATE_W, MXU)],
        out_shape=[jax.ShapeDtypeStruct((DEPTH, N_GROUPS, MXU, MXU), _BF),
                   jax.ShapeDtypeStruct((DEPTH, N_GROUPS, MXU, STATE_W), _BF),
                   jax.ShapeDtypeStruct((DEPTH, N_GROUPS, STATE_W, MXU), _BF)],
        compiler_params=pltpu.CompilerParams(
            dimension_semantics=("arbitrary",), vmem_limit_bytes=VMEM_LIMIT),
        name="s5_operators",
    )(tab, bt0, cx)

    flat = lambda t: t.reshape(DEPTH, 1, N_GROUPS * STATE_W)
    mult_rows = lambda n: jnp.concatenate([flat(pw1[:, :, n]), flat(pw2[:, :, n])], axis=1)
    return dict(toe=toe, w_sin=w_sin, w_sout=w_sout, a_lc=mult_rows(LC), a_1=mult_rows(1))


def _seq_prompt_kernel(x_ref, gmix_ref, wu_ref, toe_ref, wsin_ref, wsout_ref, alc_ref, d_ref,
                       gy_ref, hfin_ref,
                       u_scr, lhs_scr, e_scr, hs_scr, yg_scr, y_scr, hc_scr, *, tt):
    nb = x_ref.shape[0]
    rows = nb * tt
    m = rows // LC
    nk = tt // LC

    @pl.when(pl.program_id(0) == 0)
    def _():
        hc_scr[...] = jnp.zeros_like(hc_scr)

    x = x_ref[...].reshape(rows, D_MODEL)
    xn = _rms(x, gmix_ref[...]).astype(_BF)
    u = _dot(xn, wu_ref[...])
    for v in range(D_SSM // LANES):
        u_scr[v] = u[:, LANES * v:LANES * (v + 1)]

    slot = lax.broadcasted_iota(jnp.int32, (m, LANES), 1) // GROUP

    for v in range(D_SSM // LANES):
        rolled = []
        for j in range(LC):
            piece = u_scr[v, pl.ds(j, m, stride=LC), :]
            r = j % SLOTS
            rolled.append(pltpu.roll(piece, GROUP * r, 1) if r else piece)
        for gl in range(SLOTS):
            halves = []
            for h in range(LC // SLOTS):
                acc = rolled[SLOTS * h]
                for jl in range(1, SLOTS):
                    acc = jnp.where(slot == (gl + jl) % SLOTS, rolled[SLOTS * h + jl], acc)
                halves.append(acc)
            lhs_scr[SLOTS * v + gl] = jnp.concatenate(halves, axis=1).astype(_BF)

    for g in range(N_GROUPS):
        e_scr[g] = _dot(lhs_scr[g], wsin_ref[g])

    for g in range(N_GROUPS):
        sl = pl.ds(STATE_W * g, STATE_W)
        m1 = alc_ref[0:1, sl]
        m2 = alc_ref[1:2, sl]
        h = hc_scr[:, sl]
        for k in range(nk):
            hs_scr[g, pl.ds(k, nb, stride=nk), :] = h
            e = e_scr[g, pl.ds(k, nb, stride=nk), :]
            h = h * m1 + pltpu.roll(h, P_STATE, 1) * m2 + e
        hc_scr[:, sl] = h
        hfin_ref[:, sl] = h

    for g in range(N_GROUPS):
        yg_scr[g] = _dot(lhs_scr[g], toe_ref[g]) + _dot(hs_scr[g].astype(_BF), wsout_ref[g])

    for v in range(D_SSM // LANES):
        dv = d_ref[:, pl.ds(LANES * v, LANES)]
        for j in range(LC):
            hf, jl = divmod(j, SLOTS)
            acc = yg_scr[SLOTS * v, :, pl.ds(LANES * hf, LANES)]
            for gl in range(1, SLOTS):
                acc = jnp.where(slot == (gl + jl) % SLOTS,
                                yg_scr[SLOTS * v + gl, :, pl.ds(LANES * hf, LANES)], acc)
            if jl:
                acc = pltpu.roll(acc, GROUP * (SLOTS - jl), 1)
            y_scr[v, pl.ds(j, m, stride=LC), :] = acc + dv * u_scr[v, pl.ds(j, m, stride=LC), :]

    y = jnp.concatenate([y_scr[v] for v in range(D_SSM // LANES)], axis=1)
    gy_ref[...] = jax.nn.gelu(y).astype(_BF).reshape(nb, tt, D_SSM)


def _layer_const(l, shape, index=None, **kw):
    idx = (l,) + (tuple(index) if index is not None else (0,) * len(shape))
    return pl.BlockSpec((None,) + tuple(shape), lambda *_: idx, **kw)


def _seq_prompt(x, l, gmix, w_uc, s5, ssm_d):
    nb, seq, _ = x.shape
    tt = SEQ_TT
    rows = nb * tt
    m = rows // LC
    return pl.pallas_call(
        functools.partial(_seq_prompt_kernel, tt=tt),
        grid=(seq // tt,),
        in_specs=[
            pl.BlockSpec((nb, tt, D_MODEL), lambda i: (0, i, 0)),
            _layer_const(l, (1, D_MODEL)),
            _layer_const(l, (D_MODEL, D_SSM)),
            _layer_const(l, (N_GROUPS, MXU, MXU)),
            _layer_const(l, (N_GROUPS, MXU, STATE_W)),
            _layer_const(l, (N_GROUPS, STATE_W, MXU)),
            _layer_const(l, (2, N_GROUPS * STATE_W)),
            _layer_const(l, (1, D_SSM)),
        ],
        out_specs=[
            pl.BlockSpec((nb, tt, D_SSM), lambda i: (0, i, 0)),
            pl.BlockSpec((nb, N_GROUPS * STATE_W), lambda i: (0, 0)),
        ],
        out_shape=[
            jax.ShapeDtypeStruct((nb, seq, D_SSM), _BF),
            jax.ShapeDtypeStruct((nb, N_GROUPS * STATE_W), _F32),
        ],
        scratch_shapes=[
            pltpu.VMEM((D_SSM // LANES, rows, LANES), _F32),
            pltpu.VMEM((N_GROUPS, m, MXU), _BF),
            pltpu.VMEM((N_GROUPS, m, STATE_W), _F32),
            pltpu.VMEM((N_GROUPS, m, STATE_W), _F32),
            pltpu.VMEM((N_GROUPS, m, MXU), _F32),
            pltpu.VMEM((D_SSM // LANES, rows, LANES), _F32),
            pltpu.VMEM((nb, N_GROUPS * STATE_W), _F32),
        ],
        compiler_params=pltpu.CompilerParams(
            dimension_semantics=("arbitrary",), vmem_limit_bytes=VMEM_LIMIT),
        name=f"seq_prompt_l{l}",
    )(x, gmix, w_uc, s5['toe'], s5['w_sin'], s5['w_sout'], s5['a_lc'], ssm_d)


def _seq_sample_kernel(x_ref, hre_ref, him_ref, buf_ref, gmix_ref, wuc_ref, toe_ref, wsin_ref, wsout_ref,
                       a1_ref, d_ref, cw_ref, cb_ref, lng_ref, lnb_ref,
                       gy_ref, cact_ref, nre_ref, nim_ref, nbuf_ref):
    nb = x_ref.shape[0]
    xn = _rms(x_ref[...], gmix_ref[...]).astype(_BF)
    z = _dot(xn, wuc_ref[...])
    u = z[:, :D_SSM]
    c = z[:, D_SSM:D_SSM + D_CONV] * _sigmoid(z[:, D_SSM + D_CONV:])

    lane = lax.broadcasted_iota(jnp.int32, (nb, LANES), 1)
    slot = lane // GROUP
    lo = lane < P_STATE
    zero = jnp.zeros((nb, LANES), _F32)
    last = LC - 1

    ys = []
    for v in range(D_SSM // LANES):
        uv = u[:, LANES * v:LANES * (v + 1)]
        u_last = pltpu.roll(uv, GROUP * (last % SLOTS), 1)
        yv = None
        for gp in range(SLOTS // 2):
            sv = (SLOTS * v) // 2 + gp
            rv = hre_ref[:, pl.ds(LANES * sv, LANES)]
            iv = him_ref[:, pl.ds(LANES * sv, LANES)]
            h0s = (jnp.where(lo, rv, pltpu.roll(iv, P_STATE, 1)),
                   jnp.where(lo, pltpu.roll(rv, P_STATE, 1), iv))
            hn = []
            for par in range(2):
                gl = 2 * gp + par
                g = SLOTS * v + gl
                h0 = h0s[par]
                lhs_l = jnp.concatenate(
                    [zero, jnp.where(slot == (gl + last) % SLOTS, u_last, 0.0)], axis=1).astype(_BF)
                e = _dot(lhs_l, wsin_ref[g])
                sl = pl.ds(STATE_W * g, STATE_W)
                hn.append(h0 * a1_ref[0:1, sl] + pltpu.roll(h0, P_STATE, 1) * a1_ref[1:2, sl] + e)
                lhs_f = jnp.concatenate([jnp.where(slot == gl, uv, 0.0), zero], axis=1).astype(_BF)
                yg = (_dot(lhs_f, toe_ref[g, :, pl.ds(0, LANES)])
                      + _dot(h0.astype(_BF), wsout_ref[g, :, pl.ds(0, LANES)]))
                yv = yg if yv is None else jnp.where(slot == gl, yg, yv)
            nre_ref[:, pl.ds(LANES * sv, LANES)] = jnp.where(lo, hn[0], pltpu.roll(hn[1], P_STATE, 1))
            nim_ref[:, pl.ds(LANES * sv, LANES)] = jnp.where(lo, pltpu.roll(hn[0], P_STATE, 1), hn[1])
        ys.append(yv + d_ref[:, pl.ds(LANES * v, LANES)] * uv)
    gy_ref[...] = jax.nn.gelu(jnp.concatenate(ys, axis=1)).astype(_BF)

    nlb = D_CONV // LANES
    per_b = (CONV_W - 1) * nlb
    tap_rows = lambda k, lb: pl.ds(nlb * k + lb, nb, stride=per_b)
    accs = []
    for lb in range(nlb):
        ls = pl.ds(LANES * lb, LANES)
        cl = c[:, LANES * lb:LANES * (lb + 1)]
        acc = cb_ref[:, ls] + cw_ref[pl.ds(CONV_W - 1, 1), ls] * cl
        for k in range(CONV_W - 1):
            tap = buf_ref[tap_rows(k, lb), :]
            acc = acc + cw_ref[pl.ds(k, 1), ls] * tap
            if k:
                nbuf_ref[tap_rows(k - 1, lb), :] = tap
        nbuf_ref[tap_rows(CONV_W - 2, lb), :] = cl
        accs.append(acc)
    cact_ref[...] = _layernorm_silu(jnp.concatenate(accs, axis=1), lng_ref[...], lnb_ref[...]).astype(_BF)


def _seq_sample(x, h_re, h_im, buf, l, w, w_uc, s5):
    nb = x.shape[0]
    flat = nb * (CONV_W - 1) * (D_CONV // LANES)
    one = pl.Buffered(1)
    full = lambda shape: pl.BlockSpec(shape, lambda i: (0,) * len(shape))
    lc = functools.partial(_layer_const, l, pipeline_mode=one)
    return pl.pallas_call(
        _seq_sample_kernel,
        grid=(1,),
        in_specs=[
            full((nb, D_MODEL)), lc((nb, N_STATE)), lc((nb, N_STATE)), lc((flat, LANES)),
            lc((1, D_MODEL)), lc((D_MODEL, D_SSM + 2 * D_CONV)),
            lc((N_GROUPS, MXU, MXU)), lc((N_GROUPS, MXU, STATE_W)), lc((N_GROUPS, STATE_W, MXU)),
            lc((2, N_GROUPS * STATE_W)), lc((1, D_SSM)),
            lc((CONV_W, D_CONV)), lc((1, D_CONV)), lc((1, D_CONV)), lc((1, D_CONV)),
        ],
        out_specs=[full((nb, D_SSM)), full((nb, D_CONV)), full((nb, N_STATE)), full((nb, N_STATE)),
                   full((flat, LANES))],
        out_shape=[
            jax.ShapeDtypeStruct((nb, D_SSM), _BF),
            jax.ShapeDtypeStruct((nb, D_CONV), _BF),
            jax.ShapeDtypeStruct((nb, N_STATE), _F32),
            jax.ShapeDtypeStruct((nb, N_STATE), _F32),
            jax.ShapeDtypeStruct((flat, LANES), _F32),
        ],
        compiler_params=pltpu.CompilerParams(
            dimension_semantics=("arbitrary",), vmem_limit_bytes=VMEM_LIMIT),
        name=f"seq_sample_l{l}",
    )(x, h_re, h_im, buf, w['g_mix'], w_uc, s5['toe'], s5['w_sin'], s5['w_sout'], s5['a_1'], w['ssm_d'],
      w['conv_w'], w['conv_b'], w['ln_g'], w['ln_b'])


def _merge_and_ffn(x, xn, cact, gy, wgate_ref, wglu_ref, wpw_ref, wout_ref, gffn_ref,
                   wffi_ref, wffo_ref, gfin_ref, final, side_work=()):
    yg = _dot(gy, wglu_ref[...])
    ya = yg[:, :D_MODEL] * _sigmoid(yg[:, D_MODEL:])
    yb = _dot(cact, wpw_ref[...])
    ga = _sigmoid(_dot(xn, wgate_ref[:, pl.ds(0, D_MODEL)]))
    gb = _sigmoid(_dot(xn, wgate_ref[:, pl.ds(D_MODEL, D_MODEL)]))
    merged = (ga * ya + gb * yb).astype(_BF)
    x1 = x + _dot(merged, wout_ref[...])
    hn = _rms(x1, gffn_ref[...]).astype(_BF)
    acc = x1
    n_chunks = D_FF // FF_CHUNK
    per_chunk = -(-len(side_work) // n_chunks)
    for ck in range(n_chunks):
        h1 = _dot(hn, wffi_ref[:, pl.ds(FF_CHUNK * ck, FF_CHUNK)])
        h2 = _dot(hn, wffi_ref[:, pl.ds(D_FF + FF_CHUNK * ck, FF_CHUNK)])
        a = (h1 * _sigmoid(h1) * h2).astype(_BF)
        acc = acc + _dot(a, wffo_ref[pl.ds(FF_CHUNK * ck, FF_CHUNK), :])
        for thunk in side_work[per_chunk * ck:per_chunk * (ck + 1)]:
            thunk()
    if final:
        acc = _rms(acc, gfin_ref[...])
    return acc


def _conv_head(x, first, gmix_ref, wcv_ref, wcg_ref, cfull_scr, shift_scr, tm):
    cfull_scr[pl.ds(0, HIST), :] = jnp.where(first, 0.0, cfull_scr[pl.ds(0, HIST), :])
    xn = _rms(x, gmix_ref[...]).astype(_BF)
    cfull_scr[pl.ds(HIST, tm), :] = _dot(xn, wcv_ref[...]) * _sigmoid(_dot(xn, wcg_ref[...]))
    for r in range(SUBLANES):
        span = tm + SUBLANES * ((CONV_W - 1 - r) // SUBLANES)
        shift_scr[r, pl.ds(0, span), :] = cfull_scr[pl.ds(r + HIST_PAD, span), :]
    return xn


def _conv_tiles(cw_ref, cb_ref, shift_scr, conv_scr, tm):
    def tile(t0, lb):
        ls = pl.ds(LANES * lb, LANES)
        acc = jnp.broadcast_to(cb_ref[:, ls], (CONV_TS, LANES))
        for k in range(CONV_W):
            r = k % SUBLANES
            acc = acc + cw_ref[pl.ds(k, 1), ls] * shift_scr[r, pl.ds(t0 + k - r, CONV_TS), ls]
        conv_scr[pl.ds(t0, CONV_TS), ls] = acc
    return [functools.partial(tile, ts * CONV_TS, lb)
            for ts in range(tm // CONV_TS) for lb in range(D_CONV // LANES)]


def _conv_tail(lng_ref, lnb_ref, cfull_scr, conv_scr, tm):
    cact = _layernorm_silu(conv_scr[...], lng_ref[...], lnb_ref[...]).astype(_BF)
    tail = cfull_scr[pl.ds(tm, HIST), :]
    cfull_scr[pl.ds(0, HIST), :] = tail
    return cact, tail


def _tok_prompt_kernel(x_ref, xnext_ref, gy_ref, gmix_ref, wcv_ref, wcg_ref, cw_ref, cb_ref, lng_ref,
                       lnb_ref, wgate_ref, wglu_ref, wpw_ref, wout_ref, gffn_ref, wffi_ref, wffo_ref,
                       gfin_ref, o_ref, hist_ref, cfull_scr, shift_scr, conv_scr, xn_scr, cact_scr,
                       *, tm, tiles_per_seq, n_tiles, final):
    n = pl.program_id(0)
    head = functools.partial(_conv_head, gmix_ref=gmix_ref, wcv_ref=wcv_ref, wcg_ref=wcg_ref,
                             cfull_scr=cfull_scr, shift_scr=shift_scr, tm=tm)
    tiles = _conv_tiles(cw_ref, cb_ref, shift_scr, conv_scr, tm)
    tail = functools.partial(_conv_tail, lng_ref, lnb_ref, cfull_scr, conv_scr, tm)

    @pl.when(n == 0)
    def _():
        xn_scr[0] = head(x_ref[...], True)
        for thunk in tiles:
            thunk()
        cact_scr[0] = tail()[0]

    cur = n % 2
    nxt = jnp.minimum(n + 1, n_tiles - 1)
    xn_cur = xn_scr[cur]
    cact_cur = cact_scr[cur]
    xn_scr[1 - cur] = head(xnext_ref[...], nxt % tiles_per_seq == 0)
    o_ref[...] = _merge_and_ffn(x_ref[...], xn_cur, cact_cur, gy_ref[...], wgate_ref, wglu_ref,
                                wpw_ref, wout_ref, gffn_ref, wffi_ref, wffo_ref, gfin_ref, final,
                                side_work=tiles)
    cact_next, last_rows = tail()
    cact_scr[1 - cur] = cact_next
    hist_ref[...] = last_rows


def _tok_weight_specs(l):
    lc = functools.partial(_layer_const, l, pipeline_mode=pl.Buffered(1))
    return dict(
        gmix=lc((1, D_MODEL)), wgate=lc((D_MODEL, 2 * D_MODEL)),
        wglu=lc((D_SSM, 2 * D_MODEL)), wpw=lc((D_CONV, D_MODEL)),
        wout=lc((D_MODEL, D_MODEL)), gffn=lc((1, D_MODEL)), wffi=lc((D_MODEL, 2 * D_FF)),
        wffo=lc((D_FF, D_MODEL)),
        gfin=pl.BlockSpec((1, D_MODEL), lambda *_: (0, 0), pipeline_mode=pl.Buffered(1)))


def _tok_prompt(x, gy, l, w, w_uc, final):
    nb, seq, _ = x.shape
    tm = TOK_TM
    ws = _tok_weight_specs(l)
    lc = functools.partial(_layer_const, l, pipeline_mode=pl.Buffered(1))
    tps = seq // tm
    n_tiles = nb * tps
    nxt = lambda n: jnp.minimum(n + 1, n_tiles - 1)
    tile = lambda width: pl.BlockSpec((None, tm, width), lambda n: (n // tps, n % tps, 0))
    tile_next = pl.BlockSpec((None, tm, D_MODEL), lambda n: (nxt(n) // tps, nxt(n) % tps, 0))
    return pl.pallas_call(
        functools.partial(_tok_prompt_kernel, tm=tm, tiles_per_seq=tps, n_tiles=n_tiles, final=final),
        grid=(n_tiles,),
        in_specs=[tile(D_MODEL), tile_next, tile(D_SSM), ws['gmix'],
                  lc((D_MODEL, D_CONV), index=(0, D_SSM // D_CONV)),
                  lc((D_MODEL, D_CONV), index=(0, D_SSM // D_CONV + 1)),
                  lc((CONV_W, D_CONV)), lc((1, D_CONV)), lc((1, D_CONV)), lc((1, D_CONV)),
                  ws['wgate'], ws['wglu'], ws['wpw'], ws['wout'], ws['gffn'], ws['wffi'], ws['wffo'],
                  ws['gfin']],
        out_specs=[tile(D_MODEL), pl.BlockSpec((None, HIST, D_CONV), lambda n: (nxt(n) // tps, 0, 0))],
        out_shape=[jax.ShapeDtypeStruct((nb, seq, D_MODEL), _F32),
                   jax.ShapeDtypeStruct((nb, HIST, D_CONV), _F32)],
        scratch_shapes=[pltpu.VMEM((HIST + tm, D_CONV), _F32),
                        pltpu.VMEM((SUBLANES, HIST + tm, D_CONV), _F32),
                        pltpu.VMEM((tm, D_CONV), _F32),
                        pltpu.VMEM((2, tm, D_MODEL), _BF),
                        pltpu.VMEM((2, tm, D_CONV), _BF)],
        compiler_params=pltpu.CompilerParams(
            dimension_semantics=("arbitrary",), vmem_limit_bytes=VMEM_LIMIT),
        name=f"tok_prompt_l{l}",
    )(x, x, gy, w['g_mix'], w_uc, w_uc, w['conv_w'], w['conv_b'], w['ln_g'], w['ln_b'],
      w['w_gate'], w['w_glu'], w['w_pw'], w['w_out'], w['g_ffn'], w['w_ff_in'], w['w_ff_out'], w['g_final'])


def _tok_sample_kernel(x_ref, gy_ref, cact_ref, gmix_ref, wgate_ref, wglu_ref, wpw_ref, wout_ref,
                       gffn_ref, wffi_ref, wffo_ref, gfin_ref, o_ref, *, final):
    x = x_ref[...]
    xn = _rms(x, gmix_ref[...]).astype(_BF)
    o_ref[...] = _merge_and_ffn(x, xn, cact_ref[...], gy_ref[...], wgate_ref, wglu_ref, wpw_ref,
                                wout_ref, gffn_ref, wffi_ref, wffo_ref, gfin_ref, final)


def _tok_sample(x, gy, cact, l, w, final):
    nb = x.shape[0]
    ws = _tok_weight_specs(l)
    full = lambda shape: pl.BlockSpec(shape, lambda i: (0,) * len(shape))
    return pl.pallas_call(
        functools.partial(_tok_sample_kernel, final=final),
        grid=(1,),
        in_specs=[full((nb, D_MODEL)), full((nb, D_SSM)), full((nb, D_CONV)),
                  ws['gmix'], ws['wgate'], ws['wglu'], ws['wpw'], ws['wout'], ws['gffn'], ws['wffi'],
                  ws['wffo'], ws['gfin']],
        out_specs=full((nb, D_MODEL)),
        out_shape=jax.ShapeDtypeStruct((nb, D_MODEL), _F32),
        compiler_params=pltpu.CompilerParams(
            dimension_semantics=("arbitrary",), vmem_limit_bytes=VMEM_LIMIT),
        name=f"tok_sample_l{l}",
    )(x, gy, cact, w['g_mix'], w['w_gate'], w['w_glu'], w['w_pw'], w['w_out'], w['g_ffn'],
      w['w_ff_in'], w['w_ff_out'], w['g_final'])


def kernel(x_prompt, x_sample, state_ssm_re, state_ssm_im, state_conv, g_mix, w_in, ssm_a_re, ssm_a_im,
           ssm_log_dt, ssm_b_re, ssm_b_im, ssm_c_re, ssm_c_im, ssm_d, w_glu, conv_w, conv_b, conv_ln_g,
           conv_ln_b, w_pw, w_out, g_ffn, w_ff_in, w_ff_out, g_final):
    nbp = x_prompt.shape[0]
    nbs = x_sample.shape[0]
    row = lambda a: a.astype(_F32).reshape(DEPTH, 1, -1)
    s5 = _s5_params(ssm_a_re, ssm_a_im, ssm_log_dt, ssm_b_re, ssm_b_im, ssm_c_re, ssm_c_im)
    n_uc = D_SSM + 2 * D_CONV
    w_uc = w_in[:, :, :n_uc].astype(_BF)
    w = dict(
        g_mix=row(g_mix), w_gate=w_in[:, :, n_uc:].astype(_BF), w_glu=w_glu.astype(_BF),
        conv_w=conv_w.astype(_F32), conv_b=row(conv_b), ln_g=row(conv_ln_g), ln_b=row(conv_ln_b),
        w_pw=w_pw.astype(_BF), w_out=w_out.astype(_BF), g_ffn=row(g_ffn),
        w_ff_in=w_ff_in.astype(_BF), w_ff_out=w_ff_out.astype(_BF),
        g_final=g_final.astype(_F32).reshape(1, D_MODEL), ssm_d=row(ssm_d))

    xp = x_prompt
    xs = x_sample.reshape(nbs, D_MODEL)
    h_re = state_ssm_re.reshape(DEPTH, nbs, N_STATE)
    h_im = state_ssm_im.reshape(DEPTH, nbs, N_STATE)
    buf = state_conv.reshape(DEPTH, -1, LANES)

    re_p, im_p, conv_p, re_s, im_s, conv_s = [], [], [], [], [], []
    for l in range(DEPTH):
        final = l == DEPTH - 1
        gy, hfin = _seq_prompt(xp, l, w['g_mix'], w_uc, s5, w['ssm_d'])
        xp, hist = _tok_prompt(xp, gy, l, w, w_uc, final)
        hfin = hfin.reshape(nbp, N_GROUPS, 2, P_STATE)
        re_p.append(hfin[:, :, 0])
        im_p.append(hfin[:, :, 1])
        conv_p.append(hist[:, HIST_PAD:])

        gys, cacts, nre, nim, nbuf = _seq_sample(xs, h_re, h_im, buf, l, w, w_uc, s5)
        xs = _tok_sample(xs, gys, cacts, l, w, final)
        re_s.append(nre.reshape(nbs, N_GROUPS, P_STATE))
        im_s.append(nim.reshape(nbs, N_GROUPS, P_STATE))
        conv_s.append(nbuf.reshape(nbs, CONV_W - 1, D_CONV))

    return (xp, xs.reshape(nbs, 1, D_MODEL), jnp.stack(re_p), jnp.stack(im_p), jnp.stack(conv_p),
            jnp.stack(re_s), jnp.stack(im_s), jnp.stack(conv_s))
```

```python
import functools

import numpy as np
import jax
import jax.numpy as jnp
from jax import lax
from jax.experimental import pallas as pl
from jax.experimental.pallas import tpu as pltpu

D_MODEL = 1024
DEPTH = 4
D_SSM = 512
GROUP = 16
N_GROUPS = 32
P_STATE = 64
D_CONV = 512
CONV_W = 31
D_FF = 2816
EPS = 1e-6

LANES = 128
SUBLANES = 8
MXU = 256
LC = 16
SLOTS = LANES // GROUP
STATE_W = 2 * P_STATE
N_STATE = N_GROUPS * P_STATE
HIST = 32
HIST_PAD = HIST - (CONV_W - 1)
FF_CHUNK = MXU
VMEM_LIMIT = 58 * 1024 * 1024

SEQ_TT = 128
TOK_TM = 512
CONV_TS = 64

_BF = jnp.bfloat16
_F32 = jnp.float32


def _dot(a, b):
    return jnp.dot(a, b, preferred_element_type=_F32)


def _rms(xf, g):
    return xf * lax.rsqrt(jnp.mean(xf * xf, axis=-1, keepdims=True) + EPS) * g


def _sigmoid(x):
    return jax.nn.sigmoid(x)


def _layernorm_silu(c, g, b):
    mu = jnp.mean(c, axis=-1, keepdims=True)
    var = jnp.mean(jnp.square(c - mu), axis=-1, keepdims=True)
    y = (c - mu) * lax.rsqrt(var + EPS) * g + b
    return y * _sigmoid(y)


N_POW = LC + 1
ROW_PW1 = 0
ROW_PW2 = N_POW
ROW_Q1 = 2 * N_POW
ROW_Q2 = 2 * N_POW + 1
TAB_ROWS = 40


def _swap(v):
    return pltpu.roll(v, P_STATE, 1)


def _s5_ops_kernel(tab_ref, bt_ref, cx_ref, toe_ref, wsin_ref, wsout_ref):
    lane = lax.broadcasted_iota(jnp.int32, (GROUP, LANES), 1)
    zeros = jnp.zeros((GROUP, LANES), _F32)

    def one_group(g, g8):
        row = lambda r: tab_ref[g, pl.ds(r, 1), :]
        bt0 = bt_ref[g]
        bbar = bt0 * row(ROW_Q1) + _swap(bt0) * row(ROW_Q2)
        bbar_s = _swap(bbar)
        cx = cx_ref[g]
        cxs = _swap(cx)
        ca = [cx * row(ROW_PW1 + n) - cxs * row(ROW_PW2 + n) for n in range(N_POW)]
        kt = lax.dot_general(bbar, jnp.concatenate(ca[:LC], axis=0), (((1,), (1,)), ((), ())),
                             precision=lax.Precision.HIGHEST, preferred_element_type=_F32)
        kt0, kt1 = kt[:, :LANES], kt[:, LANES:]
        for j in range(LC):
            jh, jl = divmod(j, SLOTS)
            off = LANES * jh + GROUP * ((jl + g8) % SLOTS)
            r0 = pltpu.roll(kt0, GROUP * jl, 1) if jl else kt0
            r1 = pltpu.roll(kt1, GROUP * jl, 1) if jl else kt1
            lo = lane < GROUP * jl
            if jh == 0:
                o0 = jnp.where(lo, 0.0, r0) if jl else r0
                o1 = jnp.where(lo, r0, r1) if jl else r1
            else:
                o0 = zeros
                o1 = jnp.where(lo, 0.0, r0) if jl else r0
            if g8:
                o0 = pltpu.roll(o0, GROUP * g8, 1)
                o1 = pltpu.roll(o1, GROUP * g8, 1)
            toe_ref[g, pl.ds(off, GROUP), :] = jnp.concatenate([o0, o1], axis=1).astype(_BF)
            n = LC - 1 - j
            wsin_ref[g, pl.ds(off, GROUP), :] = (
                bbar * row(ROW_PW1 + n) + bbar_s * row(ROW_PW2 + n)).astype(_BF)
        wt = jnp.concatenate(ca[1:], axis=0).T
        w0, w1 = wt[:, :LANES], wt[:, LANES:]
        if g8:
            w0 = pltpu.roll(w0, GROUP * g8, 1)
            w1 = pltpu.roll(w1, GROUP * g8, 1)
        wsout_ref[g] = jnp.concatenate([w0, w1], axis=1).astype(_BF)

    def octet(q, carry):
        for g8 in range(SLOTS):
            one_group(q * SLOTS + g8, g8)
        return carry

    lax.fori_loop(0, N_GROUPS // SLOTS, octet, 0)


def _s5_params(a_re, a_im, log_dt, b_re, b_im, c_re, c_im):
    a_re, a_im = a_re.astype(_F32), a_im.astype(_F32)
    dt = jnp.exp(log_dt.astype(_F32))[..., None]
    lre, lim = dt * a_re, dt * a_im
    n = jnp.arange(LC + 1, dtype=_F32)[None, None, :, None]
    pmag = jnp.exp(n * lre[:, :, None, :])
    pw_re = pmag * jnp.cos(n * lim[:, :, None, :])
    pw_im = pmag * jnp.sin(n * lim[:, :, None, :])
    ab_re, ab_im = pw_re[:, :, 1], pw_im[:, :, 1]
    nr, ni = ab_re - 1.0, ab_im
    den = a_re * a_re + a_im * a_im
    q_re = (nr * a_re + ni * a_im) / den
    q_im = (ni * a_re - nr * a_im) / den
    both = lambda a, b: jnp.concatenate([a, b], axis=-1)
    pw1 = both(pw_re, pw_re)
    pw2 = both(-pw_im, pw_im)
    tab = jnp.concatenate(
        [pw1, pw2, both(q_re, q_re)[:, :, None], both(-q_im, q_im)[:, :, None],
         jnp.zeros((DEPTH, N_GROUPS, TAB_ROWS - 2 * N_POW - 2, STATE_W), _F32)], axis=2)
    bt0 = both(b_re.astype(_F32).transpose(0, 1, 3, 2), b_im.astype(_F32).transpose(0, 1, 3, 2))
    cx = both(c_re.astype(_F32), -c_im.astype(_F32))

    per_layer = lambda *shape: pl.BlockSpec((None,) + shape, lambda l: (l,) + (0,) * len(shape))
    toe, w_sin, w_sout = pl.pallas_call(
        _s5_ops_kernel,
        grid=(DEPTH,),
        in_specs=[per_layer(N_GROUPS, TAB_ROWS, STATE_W), per_layer(N_GROUPS, GROUP, STATE_W),
                  per_layer(N_GROUPS, GROUP, STATE_W)],
        out_specs=[per_layer(N_GROUPS, MXU, MXU), per_layer(N_GROUPS, MXU, STATE_W),
                   per_layer(N_GROUPS, STATE_W, MXU)],
        out_shape=[jax.ShapeDtypeStruct((DEPTH, N_GROUPS, MXU, MXU), _BF),
                   jax.ShapeDtypeStruct((DEPTH, N_GROUPS, MXU, STATE_W), _BF),
                   jax.ShapeDtypeStruct((DEPTH, N_GROUPS, STATE_W, MXU), _BF)],
        compiler_params=pltpu.CompilerParams(
            dimension_semantics=("arbitrary",), vmem_limit_bytes=VMEM_LIMIT),
        name="s5_operators",
    )(tab, bt0, cx)

    flat = lambda t: t.reshape(DEPTH, 1, N_GROUPS * STATE_W)
    mult_rows = lambda n: jnp.concatenate([flat(pw1[:, :, n]), flat(pw2[:, :, n])], axis=1)
    return dict(toe=toe, w_sin=w_sin, w_sout=w_sout, a_lc=mult_rows(LC), a_1=mult_rows(1))


def _seq_prompt_kernel(x_ref, gmix_ref, wu_ref, toe_ref, wsin_ref, wsout_ref, alc_ref, d_ref,
                       gy_ref, hfin_ref,
                       u_scr, lhs_scr, e_scr, hs_scr, yg_scr, y_scr, hc_scr, *, tt):
    rows = x_ref.shape[0]
    nb = rows // tt
    m = rows // LC
    nk = tt // LC
    step_rows = lambda k, j: pl.ds((LC * k + j) * nb, nb)

    @pl.when(pl.program_id(0) == 0)
    def _():
        hc_scr[...] = jnp.zeros_like(hc_scr)

    xn = _rms(x_ref[...], gmix_ref[...]).astype(_BF)
    u_scr[...] = _dot(xn, wu_ref[...])

    slot = lax.broadcasted_iota(jnp.int32, (m, LANES), 1) // GROUP

    for v in range(D_SSM // LANES):
        ls = pl.ds(LANES * v, LANES)
        rolled = []
        for j in range(LC):
            piece = jnp.concatenate([u_scr[step_rows(k, j), ls] for k in range(nk)], axis=0)
            r = j % SLOTS
            rolled.append(pltpu.roll(piece, GROUP * r, 1) if r else piece)
        for gl in range(SLOTS):
            halves = []
            for h in range(LC // SLOTS):
                acc = rolled[SLOTS * h]
                for jl in range(1, SLOTS):
                    acc = jnp.where(slot == (gl + jl) % SLOTS, rolled[SLOTS * h + jl], acc)
                halves.append(acc)
            lhs_scr[SLOTS * v + gl] = jnp.concatenate(halves, axis=1).astype(_BF)

    for g in range(N_GROUPS):
        e_scr[g] = _dot(lhs_scr[g], wsin_ref[g])

    for g in range(N_GROUPS):
        sl = pl.ds(STATE_W * g, STATE_W)
        m1 = alc_ref[0:1, sl]
        m2 = alc_ref[1:2, sl]
        h = hc_scr[:, sl]
        for k in range(nk):
            hs_scr[g, pl.ds(nb * k, nb), :] = h
            h = h * m1 + pltpu.roll(h, P_STATE, 1) * m2 + e_scr[g, pl.ds(nb * k, nb), :]
        hc_scr[:, sl] = h
        hfin_ref[:, sl] = h

    for g in range(N_GROUPS):
        yg_scr[g] = _dot(lhs_scr[g], toe_ref[g]) + _dot(hs_scr[g].astype(_BF), wsout_ref[g])

    for v in range(D_SSM // LANES):
        ls = pl.ds(LANES * v, LANES)
        dv = d_ref[:, ls]
        for j in range(LC):
            hf, jl = divmod(j, SLOTS)
            acc = yg_scr[SLOTS * v, :, pl.ds(LANES * hf, LANES)]
            for gl in range(1, SLOTS):
                acc = jnp.where(slot == (gl + jl) % SLOTS,
                                yg_scr[SLOTS * v + gl, :, pl.ds(LANES * hf, LANES)], acc)
            if jl:
                acc = pltpu.roll(acc, GROUP * (SLOTS - jl), 1)
            for k in range(nk):
                y_scr[step_rows(k, j), ls] = acc[nb * k:nb * (k + 1)] + dv * u_scr[step_rows(k, j), ls]

    gy_ref[...] = jax.nn.gelu(y_scr[...]).astype(_BF)


def _layer_const(l, shape, index=None, **kw):
    idx = (l,) + (tuple(index) if index is not None else (0,) * len(shape))
    return pl.BlockSpec((None,) + tuple(shape), lambda *_: idx, **kw)


def _seq_prompt(x, nb, l, gmix, w_uc, s5, ssm_d):
    assert nb == SUBLANES
    seq = x.shape[0] // nb
    tt = SEQ_TT
    rows = nb * tt
    m = rows // LC
    return pl.pallas_call(
        functools.partial(_seq_prompt_kernel, tt=tt),
        grid=(seq // tt,),
        in_specs=[
            pl.BlockSpec((rows, D_MODEL), lambda i: (i, 0)),
            _layer_const(l, (1, D_MODEL)),
            _layer_const(l, (D_MODEL, D_SSM)),
            _layer_const(l, (N_GROUPS, MXU, MXU)),
            _layer_const(l, (N_GROUPS, MXU, STATE_W)),
            _layer_const(l, (N_GROUPS, STATE_W, MXU)),
            _layer_const(l, (2, N_GROUPS * STATE_W)),
            _layer_const(l, (1, D_SSM)),
        ],
        out_specs=[
            pl.BlockSpec((rows, D_SSM), lambda i: (i, 0)),
            pl.BlockSpec((nb, N_GROUPS * STATE_W), lambda i: (0, 0)),
        ],
        out_shape=[
            jax.ShapeDtypeStruct((seq * nb, D_SSM), _BF),
            jax.ShapeDtypeStruct((nb, N_GROUPS * STATE_W), _F32),
        ],
        scratch_shapes=[
            pltpu.VMEM((rows, D_SSM), _F32),
            pltpu.VMEM((N_GROUPS, m, MXU), _BF),
            pltpu.VMEM((N_GROUPS, m, STATE_W), _F32),
            pltpu.VMEM((N_GROUPS, m, STATE_W), _F32),
            pltpu.VMEM((N_GROUPS, m, MXU), _F32),
            pltpu.VMEM((rows, D_SSM), _F32),
            pltpu.VMEM((nb, N_GROUPS * STATE_W), _F32),
        ],
        compiler_params=pltpu.CompilerParams(
            dimension_semantics=("arbitrary",), vmem_limit_bytes=VMEM_LIMIT),
        name=f"seq_prompt_l{l}",
    )(x, gmix, w_uc, s5['toe'], s5['w_sin'], s5['w_sout'], s5['a_lc'], ssm_d)


def _seq_sample_kernel(x_ref, hre_ref, him_ref, buf_ref, gmix_ref, wuc_ref, toe_ref, wsin_ref, wsout_ref,
                       a1_ref, d_ref, cw_ref, cb_ref, lng_ref, lnb_ref,
                       gy_ref, cact_ref, nre_ref, nim_ref, nbuf_ref):
    nb = x_ref.shape[0]
    xn = _rms(x_ref[...], gmix_ref[...]).astype(_BF)
    z = _dot(xn, wuc_ref[...])
    u = z[:, :D_SSM]
    c = z[:, D_SSM:D_SSM + D_CONV] * _sigmoid(z[:, D_SSM + D_CONV:])

    lane = lax.broadcasted_iota(jnp.int32, (nb, LANES), 1)
    slot = lane // GROUP
    lo = lane < P_STATE
    zero = jnp.zeros((nb, LANES), _F32)
    last = LC - 1

    ys = []
    for v in range(D_SSM // LANES):
        uv = u[:, LANES * v:LANES * (v + 1)]
        u_last = pltpu.roll(uv, GROUP * (last % SLOTS), 1)
        yv = None
        for gp in range(SLOTS // 2):
            sv = (SLOTS * v) // 2 + gp
            rv = hre_ref[:, pl.ds(LANES * sv, LANES)]
            iv = him_ref[:, pl.ds(LANES * sv, LANES)]
            h0s = (jnp.where(lo, rv, pltpu.roll(iv, P_STATE, 1)),
                   jnp.where(lo, pltpu.roll(rv, P_STATE, 1), iv))
            hn = []
            for par in range(2):
                gl = 2 * gp + par
                g = SLOTS * v + gl
                h0 = h0s[par]
                lhs_l = jnp.concatenate(
                    [zero, jnp.where(slot == (gl + last) % SLOTS, u_last, 0.0)], axis=1).astype(_BF)
                e = _dot(lhs_l, wsin_ref[g])
                sl = pl.ds(STATE_W * g, STATE_W)
                hn.append(h0 * a1_ref[0:1, sl] + pltpu.roll(h0, P_STATE, 1) * a1_ref[1:2, sl] + e)
                lhs_f = jnp.concatenate([jnp.where(slot == gl, uv, 0.0), zero], axis=1).astype(_BF)
                yg = (_dot(lhs_f, toe_ref[g, :, pl.ds(0, LANES)])
                      + _dot(h0.astype(_BF), wsout_ref[g, :, pl.ds(0, LANES)]))
                yv = yg if yv is None else jnp.where(slot == gl, yg, yv)
            nre_ref[:, pl.ds(LANES * sv, LANES)] = jnp.where(lo, hn[0], pltpu.roll(hn[1], P_STATE, 1))
            nim_ref[:, pl.ds(LANES * sv, LANES)] = jnp.where(lo, pltpu.roll(hn[0], P_STATE, 1), hn[1])
        ys.append(yv + d_ref[:, pl.ds(LANES * v, LANES)] * uv)
    gy_ref[...] = jax.nn.gelu(jnp.concatenate(ys, axis=1)).astype(_BF)

    nlb = D_CONV // LANES
    per_b = (CONV_W - 1) * nlb
    tap_rows = lambda k, lb: pl.ds(nlb * k + lb, nb, stride=per_b)
    accs = []
    for lb in range(nlb):
        ls = pl.ds(LANES * lb, LANES)
        cl = c[:, LANES * lb:LANES * (lb + 1)]
        acc = cb_ref[:, ls] + cw_ref[pl.ds(CONV_W - 1, 1), ls] * cl
        for k in range(CONV_W - 1):
            tap = buf_ref[tap_rows(k, lb), :]
            acc = acc + cw_ref[pl.ds(k, 1), ls] * tap
            if k:
                nbuf_ref[tap_rows(k - 1, lb), :] = tap
        nbuf_ref[tap_rows(CONV_W - 2, lb), :] = cl
        accs.append(acc)
    cact_ref[...] = _layernorm_silu(jnp.concatenate(accs, axis=1), lng_ref[...], lnb_ref[...]).astype(_BF)


def _seq_sample(x, h_re, h_im, buf, l, w, w_uc, s5):
    nb = x.shape[0]
    flat = nb * (CONV_W - 1) * (D_CONV // LANES)
    one = pl.Buffered(1)
    full = lambda shape: pl.BlockSpec(shape, lambda i: (0,) * len(shape))
    lc = functools.partial(_layer_const, l, pipeline_mode=one)
    return pl.pallas_call(
        _seq_sample_kernel,
        grid=(1,),
        in_specs=[
            full((nb, D_MODEL)), lc((nb, N_STATE)), lc((nb, N_STATE)), lc((flat, LANES)),
            lc((1, D_MODEL)), lc((D_MODEL, D_SSM + 2 * D_CONV)),
            lc((N_GROUPS, MXU, MXU)), lc((N_GROUPS, MXU, STATE_W)), lc((N_GROUPS, STATE_W, MXU)),
            lc((2, N_GROUPS * STATE_W)), lc((1, D_SSM)),
            lc((CONV_W, D_CONV)), lc((1, D_CONV)), lc((1, D_CONV)), lc((1, D_CONV)),
        ],
        out_specs=[full((nb, D_SSM)), full((nb, D_CONV)), full((nb, N_STATE)), full((nb, N_STATE)),
                   full((flat, LANES))],
        out_shape=[
            jax.ShapeDtypeStruct((nb, D_SSM), _BF),
            jax.ShapeDtypeStruct((nb, D_CONV), _BF),
            jax.ShapeDtypeStruct((nb, N_STATE), _F32),
            jax.ShapeDtypeStruct((nb, N_STATE), _F32),
            jax.ShapeDtypeStruct((flat, LANES), _F32),
        ],
        compiler_params=pltpu.CompilerParams(
            dimension_semantics=("arbitrary",), vmem_limit_bytes=VMEM_LIMIT),
        name=f"seq_sample_l{l}",
    )(x, h_re, h_im, buf, w['g_mix'], w_uc, s5['toe'], s5['w_sin'], s5['w_sout'], s5['a_1'], w['ssm_d'],
      w['conv_w'], w['conv_b'], w['ln_g'], w['ln_b'])


def _merge_and_ffn(x, xn, cact, gy, wgate_ref, wglu_ref, wpw_ref, wout_ref, gffn_ref,
                   wffi_ref, wffo_ref, gfin_ref, final, side_work=()):
    yg = _dot(gy, wglu_ref[...])
    ya = yg[:, :D_MODEL] * _sigmoid(yg[:, D_MODEL:])
    yb = _dot(cact, wpw_ref[...])
    ga = _sigmoid(_dot(xn, wgate_ref[:, pl.ds(0, D_MODEL)]))
    gb = _sigmoid(_dot(xn, wgate_ref[:, pl.ds(D_MODEL, D_MODEL)]))
    merged = (ga * ya + gb * yb).astype(_BF)
    x1 = x + _dot(merged, wout_ref[...])
    hn = _rms(x1, gffn_ref[...]).astype(_BF)
    acc = x1
    n_chunks = D_FF // FF_CHUNK
    per_chunk = -(-len(side_work) // n_chunks)
    for ck in range(n_chunks):
        h1 = _dot(hn, wffi_ref[:, pl.ds(FF_CHUNK * ck, FF_CHUNK)])
        h2 = _dot(hn, wffi_ref[:, pl.ds(D_FF + FF_CHUNK * ck, FF_CHUNK)])
        a = (h1 * _sigmoid(h1) * h2).astype(_BF)
        acc = acc + _dot(a, wffo_ref[pl.ds(FF_CHUNK * ck, FF_CHUNK), :])
        for thunk in side_work[per_chunk * ck:per_chunk * (ck + 1)]:
            thunk()
    if final:
        acc = _rms(acc, gfin_ref[...])
    return acc


def _conv_head(x, gmix_ref, wcv_ref, wcg_ref, cfull_scr, tm, hist_rows):
    xn = _rms(x, gmix_ref[...]).astype(_BF)
    cfull_scr[pl.ds(hist_rows, tm), :] = _dot(xn, wcv_ref[...]) * _sigmoid(_dot(xn, wcg_ref[...]))
    return xn


def _conv_tiles(cw_ref, cb_ref, cfull_scr, conv_scr, tm, nb):
    def tile(t0, lb):
        ls = pl.ds(LANES * lb, LANES)
        acc = jnp.broadcast_to(cb_ref[:, ls], (CONV_TS, LANES))
        for k in range(CONV_W):
            acc = acc + cw_ref[pl.ds(k, 1), ls] * cfull_scr[pl.ds(t0 + nb * (k + HIST_PAD), CONV_TS), ls]
        conv_scr[pl.ds(t0, CONV_TS), ls] = acc
    return [functools.partial(tile, ts * CONV_TS, lb)
            for ts in range(tm // CONV_TS) for lb in range(D_CONV // LANES)]


def _conv_tail(lng_ref, lnb_ref, cfull_scr, conv_scr, tm, hist_rows):
    cact = _layernorm_silu(conv_scr[...], lng_ref[...], lnb_ref[...]).astype(_BF)
    tail = cfull_scr[pl.ds(tm, hist_rows), :]
    cfull_scr[pl.ds(0, hist_rows), :] = tail
    return cact, tail


def _tok_prompt_kernel(x_ref, xnext_ref, gy_ref, gmix_ref, wcv_ref, wcg_ref, cw_ref, cb_ref, lng_ref,
                       lnb_ref, wgate_ref, wglu_ref, wpw_ref, wout_ref, gffn_ref, wffi_ref, wffo_ref,
                       gfin_ref, o_ref, hist_ref, cfull_scr, conv_scr, xn_scr, cact_scr,
                       *, tm, nb, final):
    n = pl.program_id(0)
    hist_rows = HIST * nb
    head = functools.partial(_conv_head, gmix_ref=gmix_ref, wcv_ref=wcv_ref, wcg_ref=wcg_ref,
                             cfull_scr=cfull_scr, tm=tm, hist_rows=hist_rows)
    tiles = _conv_tiles(cw_ref, cb_ref, cfull_scr, conv_scr, tm, nb)
    tail = functools.partial(_conv_tail, lng_ref, lnb_ref, cfull_scr, conv_scr, tm, hist_rows)

    @pl.when(n == 0)
    def _():
        cfull_scr[pl.ds(0, hist_rows), :] = jnp.zeros((hist_rows, D_CONV), _F32)
        xn_scr[0] = head(x_ref[...])
        for thunk in tiles:
            thunk()
        cact_scr[0] = tail()[0]

    cur = n % 2
    xn_cur = xn_scr[cur]
    cact_cur = cact_scr[cur]
    xn_scr[1 - cur] = head(xnext_ref[...])
    o_ref[...] = _merge_and_ffn(x_ref[...], xn_cur, cact_cur, gy_ref[...], wgate_ref, wglu_ref,
                                wpw_ref, wout_ref, gffn_ref, wffi_ref, wffo_ref, gfin_ref, final,
                                side_work=tiles)
    cact_next, last_rows = tail()
    cact_scr[1 - cur] = cact_next
    hist_ref[...] = last_rows


def _tok_weight_specs(l):
    lc = functools.partial(_layer_const, l, pipeline_mode=pl.Buffered(1))
    return dict(
        gmix=lc((1, D_MODEL)), wgate=lc((D_MODEL, 2 * D_MODEL)),
        wglu=lc((D_SSM, 2 * D_MODEL)), wpw=lc((D_CONV, D_MODEL)),
        wout=lc((D_MODEL, D_MODEL)), gffn=lc((1, D_MODEL)), wffi=lc((D_MODEL, 2 * D_FF)),
        wffo=lc((D_FF, D_MODEL)),
        gfin=pl.BlockSpec((1, D_MODEL), lambda *_: (0, 0), pipeline_mode=pl.Buffered(1)))


def _tok_prompt(x, gy, nb, l, w, w_uc, final):
    total = x.shape[0]
    tm = TOK_TM
    hist_rows = HIST * nb
    assert tm >= hist_rows and tm % CONV_TS == 0 and total % tm == 0
    ws = _tok_weight_specs(l)
    lc = functools.partial(_layer_const, l, pipeline_mode=pl.Buffered(1))
    n_tiles = total // tm
    tile = lambda width: pl.BlockSpec((tm, width), lambda n: (n, 0))
    tile_next = pl.BlockSpec((tm, D_MODEL), lambda n: (jnp.minimum(n + 1, n_tiles - 1), 0))
    return pl.pallas_call(
        functools.partial(_tok_prompt_kernel, tm=tm, nb=nb, final=final),
        grid=(n_tiles,),
        in_specs=[tile(D_MODEL), tile_next, tile(D_SSM), ws['gmix'],
                  lc((D_MODEL, D_CONV), index=(0, D_SSM // D_CONV)),
                  lc((D_MODEL, D_CONV), index=(0, D_SSM // D_CONV + 1)),
                  lc((CONV_W, D_CONV)), lc((1, D_CONV)), lc((1, D_CONV)), lc((1, D_CONV)),
                  ws['wgate'], ws['wglu'], ws['wpw'], ws['wout'], ws['gffn'], ws['wffi'], ws['wffo'],
                  ws['gfin']],
        out_specs=[tile(D_MODEL), pl.BlockSpec((hist_rows, D_CONV), lambda n: (0, 0))],
        out_shape=[jax.ShapeDtypeStruct((total, D_MODEL), _F32),
                   jax.ShapeDtypeStruct((hist_rows, D_CONV), _F32)],
        scratch_shapes=[pltpu.VMEM((hist_rows + tm, D_CONV), _F32),
                        pltpu.VMEM((tm, D_CONV), _F32),
                        pltpu.VMEM((2, tm, D_MODEL), _BF),
                        pltpu.VMEM((2, tm, D_CONV), _BF)],
        compiler_params=pltpu.CompilerParams(
            dimension_semantics=("arbitrary",), vmem_limit_bytes=VMEM_LIMIT),
        name=f"tok_prompt_l{l}",
    )(x, x, gy, w['g_mix'], w_uc, w_uc, w['conv_w'], w['conv_b'], w['ln_g'], w['ln_b'],
      w['w_gate'], w['w_glu'], w['w_pw'], w['w_out'], w['g_ffn'], w['w_ff_in'], w['w_ff_out'], w['g_final'])


def _tok_sample_kernel(x_ref, gy_ref, cact_ref, gmix_ref, wgate_ref, wglu_ref, wpw_ref, wout_ref,
                       gffn_ref, wffi_ref, wffo_ref, gfin_ref, o_ref, *, final):
    x = x_ref[...]
    xn = _rms(x, gmix_ref[...]).astype(_BF)
    o_ref[...] = _merge_and_ffn(x, xn, cact_ref[...], gy_ref[...], wgate_ref, wglu_ref, wpw_ref,
                                wout_ref, gffn_ref, wffi_ref, wffo_ref, gfin_ref, final)


def _tok_sample(x, gy, cact, l, w, final):
    nb = x.shape[0]
    ws = _tok_weight_specs(l)
    full = lambda shape: pl.BlockSpec(shape, lambda i: (0,) * len(shape))
    return pl.pallas_call(
        functools.partial(_tok_sample_kernel, final=final),
        grid=(1,),
        in_specs=[full((nb, D_MODEL)), full((nb, D_SSM)), full((nb, D_CONV)),
                  ws['gmix'], ws['wgate'], ws['wglu'], ws['wpw'], ws['wout'], ws['gffn'], ws['wffi'],
                  ws['wffo'], ws['gfin']],
        out_specs=full((nb, D_MODEL)),
        out_shape=jax.ShapeDtypeStruct((nb, D_MODEL), _F32),
        compiler_params=pltpu.CompilerParams(
            dimension_semantics=("arbitrary",), vmem_limit_bytes=VMEM_LIMIT),
        name=f"tok_sample_l{l}",
    )(x, gy, cact, w['g_mix'], w['w_gate'], w['w_glu'], w['w_pw'], w['w_out'], w['g_ffn'],
      w['w_ff_in'], w['w_ff_out'], w['g_final'])


def kernel(x_prompt, x_sample, state_ssm_re, state_ssm_im, state_conv, g_mix, w_in, ssm_a_re, ssm_a_im,
           ssm_log_dt, ssm_b_re, ssm_b_im, ssm_c_re, ssm_c_im, ssm_d, w_glu, conv_w, conv_b, conv_ln_g,
           conv_ln_b, w_pw, w_out, g_ffn, w_ff_in, w_ff_out, g_final):
    nbp = x_prompt.shape[0]
    nbs = x_sample.shape[0]
    row = lambda a: a.astype(_F32).reshape(DEPTH, 1, -1)
    s5 = _s5_params(ssm_a_re, ssm_a_im, ssm_log_dt, ssm_b_re, ssm_b_im, ssm_c_re, ssm_c_im)
    n_uc = D_SSM + 2 * D_CONV
    w_uc = w_in[:, :, :n_uc].astype(_BF)
    w = dict(
        g_mix=row(g_mix), w_gate=w_in[:, :, n_uc:].astype(_BF), w_glu=w_glu.astype(_BF),
        conv_w=conv_w.astype(_F32), conv_b=row(conv_b), ln_g=row(conv_ln_g), ln_b=row(conv_ln_b),
        w_pw=w_pw.astype(_BF), w_out=w_out.astype(_BF), g_ffn=row(g_ffn),
        w_ff_in=w_ff_in.astype(_BF), w_ff_out=w_ff_out.astype(_BF),
        g_final=g_final.astype(_F32).reshape(1, D_MODEL), ssm_d=row(ssm_d))

    seq = x_prompt.shape[1]
    xp = x_prompt.transpose(1, 0, 2).reshape(seq * nbp, D_MODEL)
    xs = x_sample.reshape(nbs, D_MODEL)
    h_re = state_ssm_re.reshape(DEPTH, nbs, N_STATE)
    h_im = state_ssm_im.reshape(DEPTH, nbs, N_STATE)
    buf = state_conv.reshape(DEPTH, -1, LANES)

    re_p, im_p, conv_p, re_s, im_s, conv_s = [], [], [], [], [], []
    for l in range(DEPTH):
        final = l == DEPTH - 1
        gy, hfin = _seq_prompt(xp, nbp, l, w['g_mix'], w_uc, s5, w['ssm_d'])
        xp, hist = _tok_prompt(xp, gy, nbp, l, w, w_uc, final)
        hfin = hfin.reshape(nbp, N_GROUPS, 2, P_STATE)
        re_p.append(hfin[:, :, 0])
        im_p.append(hfin[:, :, 1])
        conv_p.append(hist.reshape(HIST, nbp, D_CONV)[HIST_PAD:].transpose(1, 0, 2))

        gys, cacts, nre, nim, nbuf = _seq_sample(xs, h_re, h_im, buf, l, w, w_uc, s5)
        xs = _tok_sample(xs, gys, cacts, l, w, final)
        re_s.append(nre.reshape(nbs, N_GROUPS, P_STATE))
        im_s.append(nim.reshape(nbs, N_GROUPS, P_STATE))
        conv_s.append(nbuf.reshape(nbs, CONV_W - 1, D_CONV))

    y_prompt = xp.reshape(seq, nbp, D_MODEL).transpose(1, 0, 2)
    return (y_prompt, xs.reshape(nbs, 1, D_MODEL), jnp.stack(re_p), jnp.stack(im_p), jnp.stack(conv_p),
            jnp.stack(re_s), jnp.stack(im_s), jnp.stack(conv_s))
```

```python
import functools

import numpy as np
import jax
import jax.numpy as jnp
from jax import lax
from jax.experimental import pallas as pl
from jax.experimental.pallas import tpu as pltpu

D_MODEL = 1024
DEPTH = 4
D_SSM = 512
GROUP = 16
N_GROUPS = 32
P_STATE = 64
D_CONV = 512
CONV_W = 31
D_FF = 2816
EPS = 1e-6

LANES = 128
SUBLANES = 8
MXU = 256
LC = 16
SLOTS = LANES // GROUP
STATE_W = 2 * P_STATE
N_STATE = N_GROUPS * P_STATE
HIST = 32
HIST_PAD = HIST - (CONV_W - 1)
FF_CHUNK = MXU
VMEM_LIMIT = 58 * 1024 * 1024

SEQ_TT = 256
SEQ_RB = 32
TOK_TM = 512
CONV_TS = 64

_BF = jnp.bfloat16
_F32 = jnp.float32


def _dot(a, b):
    return jnp.dot(a, b, preferred_element_type=_F32)


def _rms(xf, g):
    return xf * lax.rsqrt(jnp.mean(xf * xf, axis=-1, keepdims=True) + EPS) * g


def _sigmoid(x):
    return jax.nn.sigmoid(x)


def _layernorm_silu(c, g, b):
    mu = jnp.mean(c, axis=-1, keepdims=True)
    var = jnp.mean(jnp.square(c - mu), axis=-1, keepdims=True)
    y = (c - mu) * lax.rsqrt(var + EPS) * g + b
    return y * _sigmoid(y)


N_POW = LC + 1
ROW_PW1 = 0
ROW_PW2 = N_POW
ROW_Q1 = 2 * N_POW
ROW_Q2 = 2 * N_POW + 1
TAB_ROWS = 40


def _swap(v):
    return pltpu.roll(v, P_STATE, 1)


def _s5_ops_kernel(tab_ref, bt_ref, cx_ref, toe_ref, wsin_ref, wsout_ref):
    lane = lax.broadcasted_iota(jnp.int32, (GROUP, LANES), 1)
    zeros = jnp.zeros((GROUP, LANES), _F32)

    def one_group(g, g8):
        row = lambda r: tab_ref[g, pl.ds(r, 1), :]
        bt0 = bt_ref[g]
        bbar = bt0 * row(ROW_Q1) + _swap(bt0) * row(ROW_Q2)
        bbar_s = _swap(bbar)
        cx = cx_ref[g]
        cxs = _swap(cx)
        ca = [cx * row(ROW_PW1 + n) - cxs * row(ROW_PW2 + n) for n in range(N_POW)]
        kt = lax.dot_general(bbar, jnp.concatenate(ca[:LC], axis=0), (((1,), (1,)), ((), ())),
                             precision=lax.Precision.HIGHEST, preferred_element_type=_F32)
        kt0, kt1 = kt[:, :LANES], kt[:, LANES:]
        for j in range(LC):
            jh, jl = divmod(j, SLOTS)
            off = LANES * jh + GROUP * ((jl + g8) % SLOTS)
            r0 = pltpu.roll(kt0, GROUP * jl, 1) if jl else kt0
            r1 = pltpu.roll(kt1, GROUP * jl, 1) if jl else kt1
            lo = lane < GROUP * jl
            if jh == 0:
                o0 = jnp.where(lo, 0.0, r0) if jl else r0
                o1 = jnp.where(lo, r0, r1) if jl else r1
            else:
                o0 = zeros
                o1 = jnp.where(lo, 0.0, r0) if jl else r0
            if g8:
                o0 = pltpu.roll(o0, GROUP * g8, 1)
                o1 = pltpu.roll(o1, GROUP * g8, 1)
            toe_ref[g, pl.ds(off, GROUP), :] = jnp.concatenate([o0, o1], axis=1).astype(_BF)
            n = LC - 1 - j
            wsin_ref[g, pl.ds(off, GROUP), :] = (
                bbar * row(ROW_PW1 + n) + bbar_s * row(ROW_PW2 + n)).astype(_BF)
        wt = jnp.concatenate(ca[1:], axis=0).T
        w0, w1 = wt[:, :LANES], wt[:, LANES:]
        if g8:
            w0 = pltpu.roll(w0, GROUP * g8, 1)
            w1 = pltpu.roll(w1, GROUP * g8, 1)
        wsout_ref[g] = jnp.concatenate([w0, w1], axis=1).astype(_BF)

    def octet(q, carry):
        for g8 in range(SLOTS):
            one_group(q * SLOTS + g8, g8)
        return carry

    lax.fori_loop(0, N_GROUPS // SLOTS, octet, 0)


def _s5_params(a_re, a_im, log_dt, b_re, b_im, c_re, c_im):
    a_re, a_im = a_re.astype(_F32), a_im.astype(_F32)
    dt = jnp.exp(log_dt.astype(_F32))[..., None]
    lre, lim = dt * a_re, dt * a_im
    n = jnp.arange(LC + 1, dtype=_F32)[None, None, :, None]
    pmag = jnp.exp(n * lre[:, :, None, :])
    pw_re = pmag * jnp.cos(n * lim[:, :, None, :])
    pw_im = pmag * jnp.sin(n * lim[:, :, None, :])
    ab_re, ab_im = pw_re[:, :, 1], pw_im[:, :, 1]
    nr, ni = ab_re - 1.0, ab_im
    den = a_re * a_re + a_im * a_im
    q_re = (nr * a_re + ni * a_im) / den
    q_im = (ni * a_re - nr * a_im) / den
    both = lambda a, b: jnp.concatenate([a, b], axis=-1)
    pw1 = both(pw_re, pw_re)
    pw2 = both(-pw_im, pw_im)
    tab = jnp.concatenate(
        [pw1, pw2, both(q_re, q_re)[:, :, None], both(-q_im, q_im)[:, :, None],
         jnp.zeros((DEPTH, N_GROUPS, TAB_ROWS - 2 * N_POW - 2, STATE_W), _F32)], axis=2)
    bt0 = both(b_re.astype(_F32).transpose(0, 1, 3, 2), b_im.astype(_F32).transpose(0, 1, 3, 2))
    cx = both(c_re.astype(_F32), -c_im.astype(_F32))

    per_layer = lambda *shape: pl.BlockSpec((None,) + shape, lambda l: (l,) + (0,) * len(shape))
    toe, w_sin, w_sout = pl.pallas_call(
        _s5_ops_kernel,
        grid=(DEPTH,),
        in_specs=[per_layer(N_GROUPS, TAB_ROWS, STATE_W), per_layer(N_GROUPS, GROUP, STATE_W),
                  per_layer(N_GROUPS, GROUP, STATE_W)],
        out_specs=[per_layer(N_GROUPS, MXU, MXU), per_layer(N_GROUPS, MXU, STATE_W),
                   per_layer(N_GROUPS, STATE_W, MXU)],
        out_shape=[jax.ShapeDtypeStruct((DEPTH, N_GROUPS, MXU, MXU), _BF),
                   jax.ShapeDtypeStruct((DEPTH, N_GROUPS, MXU, STATE_W), _BF),
                   jax.ShapeDtypeStruct((DEPTH, N_GROUPS, STATE_W, MXU), _BF)],
        compiler_params=pltpu.CompilerParams(
            dimension_semantics=("arbitrary",), vmem_limit_bytes=VMEM_LIMIT),
        name="s5_operators",
    )(tab, bt0, cx)

    flat = lambda t: t.reshape(DEPTH, 1, N_GROUPS * STATE_W)
    mult_rows = lambda n: jnp.concatenate([flat(pw1[:, :, n]), flat(pw2[:, :, n])], axis=1)
    a_lc = jnp.stack([pw_re[:, :, LC].reshape(DEPTH, N_STATE), pw_im[:, :, LC].reshape(DEPTH, N_STATE)],
                     axis=1)
    return dict(toe=toe, w_sin=w_sin, w_sout=w_sout, a_lc=a_lc, a_1=mult_rows(1))


def _seq_prompt_kernel(x_ref, gmix_ref, wu_ref, toe_ref, wsin_ref, wsout_ref, alc_ref, d_ref,
                       gy_ref, hre_ref, him_ref,
                       u_scr, lhs_scr, e_scr, hs_scr, yg_scr, y_scr, hc_scr, *, tt):
    rows = x_ref.shape[0]
    nb = rows // tt
    m = rows // LC
    nk = tt // LC
    step_rows = lambda k, j: pl.ds((LC * k + j) * nb, nb)

    @pl.when(pl.program_id(0) == 0)
    def _():
        hc_scr[...] = jnp.zeros_like(hc_scr)

    xn = _rms(x_ref[...], gmix_ref[...]).astype(_BF)
    u_scr[...] = _dot(xn, wu_ref[...])

    slot = lax.broadcasted_iota(jnp.int32, (SEQ_RB, LANES), 1) // GROUP
    kb = SEQ_RB // nb

    for v in range(D_SSM // LANES):
        ls = pl.ds(LANES * v, LANES)
        for mb in range(m // SEQ_RB):
            mrows = pl.ds(SEQ_RB * mb, SEQ_RB)
            for hf in range(LC // SLOTS):
                rolled = []
                for jl in range(SLOTS):
                    piece = jnp.concatenate(
                        [u_scr[step_rows(kb * mb + k, SLOTS * hf + jl), ls] for k in range(kb)], axis=0)
                    rolled.append(pltpu.roll(piece, GROUP * jl, 1) if jl else piece)
                for gl in range(SLOTS):
                    acc = rolled[0]
                    for jl in range(1, SLOTS):
                        acc = jnp.where(slot == (gl + jl) % SLOTS, rolled[jl], acc)
                    lhs_scr[SLOTS * v + gl, mrows, pl.ds(LANES * hf, LANES)] = acc.astype(_BF)

    for g in range(N_GROUPS):
        e_scr[g] = _dot(lhs_scr[g], wsin_ref[g])

    lo = lax.broadcasted_iota(jnp.int32, (nb, LANES), 1) < P_STATE
    re_prev, im_prev = hc_scr[0], hc_scr[1]
    re_last, im_last = [], []
    for v in range(N_GROUPS // 2):
        sl = pl.ds(LANES * v, LANES)
        ar, ai = alc_ref[0:1, sl], alc_ref[1:2, sl]
        hr = re_prev[:, LANES * v:LANES * (v + 1)]
        hi = im_prev[:, LANES * v:LANES * (v + 1)]
        for k in range(nk):
            rws = pl.ds(nb * k, nb)
            e_even, e_odd = e_scr[2 * v, rws, :], e_scr[2 * v + 1, rws, :]
            e_re = jnp.where(lo, e_even, _swap(e_odd))
            e_im = jnp.where(lo, _swap(e_even), e_odd)
            hs_scr[2 * v, rws, :] = jnp.where(lo, hr, _swap(hi))
            hs_scr[2 * v + 1, rws, :] = jnp.where(lo, _swap(hr), hi)
            hr, hi = ar * hr - ai * hi + e_re, ar * hi + ai * hr + e_im
        re_last.append(hr)
        im_last.append(hi)
    re_last = jnp.concatenate(re_last, axis=1)
    im_last = jnp.concatenate(im_last, axis=1)
    hc_scr[0] = re_last
    hc_scr[1] = im_last
    hre_ref[...] = re_last
    him_ref[...] = im_last

    for g in range(N_GROUPS):
        yg_scr[g] = _dot(lhs_scr[g], toe_ref[g]) + _dot(hs_scr[g].astype(_BF), wsout_ref[g])

    for v in range(D_SSM // LANES):
        ls = pl.ds(LANES * v, LANES)
        dv = d_ref[:, ls]
        for mb in range(m // SEQ_RB):
            mrows = pl.ds(SEQ_RB * mb, SEQ_RB)
            for hf in range(LC // SLOTS):
                src = [yg_scr[SLOTS * v + gl, mrows, pl.ds(LANES * hf, LANES)] for gl in range(SLOTS)]
                for jl in range(SLOTS):
                    acc = src[0]
                    for gl in range(1, SLOTS):
                        acc = jnp.where(slot == (gl + jl) % SLOTS, src[gl], acc)
                    if jl:
                        acc = pltpu.roll(acc, GROUP * (SLOTS - jl), 1)
                    for k in range(kb):
                        rws = step_rows(kb * mb + k, SLOTS * hf + jl)
                        y_scr[rws, ls] = acc[nb * k:nb * (k + 1)] + dv * u_scr[rws, ls]

    gy_ref[...] = jax.nn.gelu(y_scr[...]).astype(_BF)


def _layer_const(l, shape, index=None, **kw):
    idx = (l,) + (tuple(index) if index is not None else (0,) * len(shape))
    return pl.BlockSpec((None,) + tuple(shape), lambda *_: idx, **kw)


def _seq_prompt(x, nb, l, gmix, w_uc, s5, ssm_d):
    assert nb == SUBLANES
    seq = x.shape[0] // nb
    tt = SEQ_TT
    rows = nb * tt
    m = rows // LC
    lc = functools.partial(_layer_const, l, pipeline_mode=pl.Buffered(1))
    return pl.pallas_call(
        functools.partial(_seq_prompt_kernel, tt=tt),
        grid=(seq // tt,),
        in_specs=[
            pl.BlockSpec((rows, D_MODEL), lambda i: (i, 0)),
            lc((1, D_MODEL)),
            lc((D_MODEL, D_SSM)),
            lc((N_GROUPS, MXU, MXU)),
            lc((N_GROUPS, MXU, STATE_W)),
            lc((N_GROUPS, STATE_W, MXU)),
            lc((2, N_STATE)),
            lc((1, D_SSM)),
        ],
        out_specs=[
            pl.BlockSpec((rows, D_SSM), lambda i: (i, 0)),
            pl.BlockSpec((nb, N_STATE), lambda i: (0, 0)),
            pl.BlockSpec((nb, N_STATE), lambda i: (0, 0)),
        ],
        out_shape=[
            jax.ShapeDtypeStruct((seq * nb, D_SSM), _BF),
            jax.ShapeDtypeStruct((nb, N_STATE), _F32),
            jax.ShapeDtypeStruct((nb, N_STATE), _F32),
        ],
        scratch_shapes=[
            pltpu.VMEM((rows, D_SSM), _F32),
            pltpu.VMEM((N_GROUPS, m, MXU), _BF),
            pltpu.VMEM((N_GROUPS, m, STATE_W), _F32),
            pltpu.VMEM((N_GROUPS, m, STATE_W), _F32),
            pltpu.VMEM((N_GROUPS, m, MXU), _F32),
            pltpu.VMEM((rows, D_SSM), _F32),
            pltpu.VMEM((2, nb, N_STATE), _F32),
        ],
        compiler_params=pltpu.CompilerParams(
            dimension_semantics=("arbitrary",), vmem_limit_bytes=VMEM_LIMIT),
        name=f"seq_prompt_l{l}",
    )(x, gmix, w_uc, s5['toe'], s5['w_sin'], s5['w_sout'], s5['a_lc'], ssm_d)


def _seq_sample_kernel(x_ref, hre_ref, him_ref, buf_ref, gmix_ref, wuc_ref, toe_ref, wsin_ref, wsout_ref,
                       a1_ref, d_ref, cw_ref, cb_ref, lng_ref, lnb_ref,
                       gy_ref, cact_ref, nre_ref, nim_ref, nbuf_ref):
    nb = x_ref.shape[0]
    xn = _rms(x_ref[...], gmix_ref[...]).astype(_BF)
    z = _dot(xn, wuc_ref[...])
    u = z[:, :D_SSM]
    c = z[:, D_SSM:D_SSM + D_CONV] * _sigmoid(z[:, D_SSM + D_CONV:])

    lane = lax.broadcasted_iota(jnp.int32, (nb, LANES), 1)
    slot = lane // GROUP
    lo = lane < P_STATE
    zero = jnp.zeros((nb, LANES), _F32)
    last = LC - 1

    ys = []
    for v in range(D_SSM // LANES):
        uv = u[:, LANES * v:LANES * (v + 1)]
        u_last = pltpu.roll(uv, GROUP * (last % SLOTS), 1)
        yv = None
        for gp in range(SLOTS // 2):
            sv = (SLOTS * v) // 2 + gp
            rv = hre_ref[:, pl.ds(LANES * sv, LANES)]
            iv = him_ref[:, pl.ds(LANES * sv, LANES)]
            h0s = (jnp.where(lo, rv, pltpu.roll(iv, P_STATE, 1)),
                   jnp.where(lo, pltpu.roll(rv, P_STATE, 1), iv))
            hn = []
            for par in range(2):
                gl = 2 * gp + par
                g = SLOTS * v + gl
                h0 = h0s[par]
                lhs_l = jnp.concatenate(
                    [zero, jnp.where(slot == (gl + last) % SLOTS, u_last, 0.0)], axis=1).astype(_BF)
                e = _dot(lhs_l, wsin_ref[g])
                sl = pl.ds(STATE_W * g, STATE_W)
                hn.append(h0 * a1_ref[0:1, sl] + pltpu.roll(h0, P_STATE, 1) * a1_ref[1:2, sl] + e)
                lhs_f = jnp.concatenate([jnp.where(slot == gl, uv, 0.0), zero], axis=1).astype(_BF)
                yg = (_dot(lhs_f, toe_ref[g, :, pl.ds(0, LANES)])
                      + _dot(h0.astype(_BF), wsout_ref[g, :, pl.ds(0, LANES)]))
                yv = yg if yv is None else jnp.where(slot == gl, yg, yv)
            nre_ref[:, pl.ds(LANES * sv, LANES)] = jnp.where(lo, hn[0], pltpu.roll(hn[1], P_STATE, 1))
            nim_ref[:, pl.ds(LANES * sv, LANES)] = jnp.where(lo, pltpu.roll(hn[0], P_STATE, 1), hn[1])
        ys.append(yv + d_ref[:, pl.ds(LANES * v, LANES)] * uv)
    gy_ref[...] = jax.nn.gelu(jnp.concatenate(ys, axis=1)).astype(_BF)

    nlb = D_CONV // LANES
    per_b = (CONV_W - 1) * nlb
    tap_rows = lambda k, lb: pl.ds(nlb * k + lb, nb, stride=per_b)
    accs = []
    for lb in range(nlb):
        ls = pl.ds(LANES * lb, LANES)
        cl = c[:, LANES * lb:LANES * (lb + 1)]
        acc = cb_ref[:, ls] + cw_ref[pl.ds(CONV_W - 1, 1), ls] * cl
        for k in range(CONV_W - 1):
            tap = buf_ref[tap_rows(k, lb), :]
            acc = acc + cw_ref[pl.ds(k, 1), ls] * tap
            if k:
                nbuf_ref[tap_rows(k - 1, lb), :] = tap
        nbuf_ref[tap_rows(CONV_W - 2, lb), :] = cl
        accs.append(acc)
    cact_ref[...] = _layernorm_silu(jnp.concatenate(accs, axis=1), lng_ref[...], lnb_ref[...]).astype(_BF)


def _seq_sample(x, h_re, h_im, buf, l, w, w_uc, s5):
    nb = x.shape[0]
    flat = nb * (CONV_W - 1) * (D_CONV // LANES)
    one = pl.Buffered(1)
    full = lambda shape: pl.BlockSpec(shape, lambda i: (0,) * len(shape))
    lc = functools.partial(_layer_const, l, pipeline_mode=one)
    return pl.pallas_call(
        _seq_sample_kernel,
        grid=(1,),
        in_specs=[
            full((nb, D_MODEL)), lc((nb, N_STATE)), lc((nb, N_STATE)), lc((flat, LANES)),
            lc((1, D_MODEL)), lc((D_MODEL, D_SSM + 2 * D_CONV)),
            lc((N_GROUPS, MXU, MXU)), lc((N_GROUPS, MXU, STATE_W)), lc((N_GROUPS, STATE_W, MXU)),
            lc((2, N_GROUPS * STATE_W)), lc((1, D_SSM)),
            lc((CONV_W, D_CONV)), lc((1, D_CONV)), lc((1, D_CONV)), lc((1, D_CONV)),
        ],
        out_specs=[full((nb, D_SSM)), full((nb, D_CONV)), full((nb, N_STATE)), full((nb, N_STATE)),
                   full((flat, LANES))],
        out_shape=[
            jax.ShapeDtypeStruct((nb, D_SSM), _BF),
            jax.ShapeDtypeStruct((nb, D_CONV), _BF),
            jax.ShapeDtypeStruct((nb, N_STATE), _F32),
            jax.ShapeDtypeStruct((nb, N_STATE), _F32),
            jax.ShapeDtypeStruct((flat, LANES), _F32),
        ],
        compiler_params=pltpu.CompilerParams(
            dimension_semantics=("arbitrary",), vmem_limit_bytes=VMEM_LIMIT),
        name=f"seq_sample_l{l}",
    )(x, h_re, h_im, buf, w['g_mix'], w_uc, s5['toe'], s5['w_sin'], s5['w_sout'], s5['a_1'], w['ssm_d'],
      w['conv_w'], w['conv_b'], w['ln_g'], w['ln_b'])


def _merge_and_ffn(x, xn, cact, gy, wgate_ref, wglu_ref, wpw_ref, wout_ref, gffn_ref,
                   wffi_ref, wffo_ref, gfin_ref, final, side_work=()):
    yg = _dot(gy, wglu_ref[...])
    ya = yg[:, :D_MODEL] * _sigmoid(yg[:, D_MODEL:])
    yb = _dot(cact, wpw_ref[...])
    ga = _sigmoid(_dot(xn, wgate_ref[:, pl.ds(0, D_MODEL)]))
    gb = _sigmoid(_dot(xn, wgate_ref[:, pl.ds(D_MODEL, D_MODEL)]))
    merged = (ga * ya + gb * yb).astype(_BF)
    x1 = x + _dot(merged, wout_ref[...])
    hn = _rms(x1, gffn_ref[...]).astype(_BF)
    acc = x1
    n_chunks = D_FF // FF_CHUNK
    per_chunk = -(-len(side_work) // n_chunks)
    for ck in range(n_chunks):
        h1 = _dot(hn, wffi_ref[:, pl.ds(FF_CHUNK * ck, FF_CHUNK)])
        h2 = _dot(hn, wffi_ref[:, pl.ds(D_FF + FF_CHUNK * ck, FF_CHUNK)])
        a = (h1 * _sigmoid(h1) * h2).astype(_BF)
        acc = acc + _dot(a, wffo_ref[pl.ds(FF_CHUNK * ck, FF_CHUNK), :])
        for thunk in side_work[per_chunk * ck:per_chunk * (ck + 1)]:
            thunk()
    if final:
        acc = _rms(acc, gfin_ref[...])
    return acc


def _conv_head(x, gmix_ref, wcv_ref, wcg_ref, cfull_scr, tm, hist_rows):
    xn = _rms(x, gmix_ref[...]).astype(_BF)
    cfull_scr[pl.ds(hist_rows, tm), :] = _dot(xn, wcv_ref[...]) * _sigmoid(_dot(xn, wcg_ref[...]))
    return xn


def _conv_tiles(cw_ref, cb_ref, cfull_scr, conv_scr, tm, nb):
    def tile(t0, lb):
        ls = pl.ds(LANES * lb, LANES)
        acc = jnp.broadcast_to(cb_ref[:, ls], (CONV_TS, LANES))
        for k in range(CONV_W):
            acc = acc + cw_ref[pl.ds(k, 1), ls] * cfull_scr[pl.ds(t0 + nb * (k + HIST_PAD), CONV_TS), ls]
        conv_scr[pl.ds(t0, CONV_TS), ls] = acc
    return [functools.partial(tile, ts * CONV_TS, lb)
            for ts in range(tm // CONV_TS) for lb in range(D_CONV // LANES)]


def _conv_tail(lng_ref, lnb_ref, cfull_scr, conv_scr, tm, hist_rows):
    cact = _layernorm_silu(conv_scr[...], lng_ref[...], lnb_ref[...]).astype(_BF)
    tail = cfull_scr[pl.ds(tm, hist_rows), :]
    cfull_scr[pl.ds(0, hist_rows), :] = tail
    return cact, tail


def _tok_prompt_kernel(x_ref, xnext_ref, gy_ref, gmix_ref, wcv_ref, wcg_ref, cw_ref, cb_ref, lng_ref,
                       lnb_ref, wgate_ref, wglu_ref, wpw_ref, wout_ref, gffn_ref, wffi_ref, wffo_ref,
                       gfin_ref, o_ref, hist_ref, cfull_scr, conv_scr, xn_scr, cact_scr,
                       *, tm, nb, final):
    n = pl.program_id(0)
    hist_rows = HIST * nb
    head = functools.partial(_conv_head, gmix_ref=gmix_ref, wcv_ref=wcv_ref, wcg_ref=wcg_ref,
                             cfull_scr=cfull_scr, tm=tm, hist_rows=hist_rows)
    tiles = _conv_tiles(cw_ref, cb_ref, cfull_scr, conv_scr, tm, nb)
    tail = functools.partial(_conv_tail, lng_ref, lnb_ref, cfull_scr, conv_scr, tm, hist_rows)

    @pl.when(n == 0)
    def _():
        cfull_scr[pl.ds(0, hist_rows), :] = jnp.zeros((hist_rows, D_CONV), _F32)
        xn_scr[0] = head(x_ref[...])
        for thunk in tiles:
            thunk()
        cact_scr[0] = tail()[0]

    cur = n % 2
    xn_cur = xn_scr[cur]
    cact_cur = cact_scr[cur]
    xn_scr[1 - cur] = head(xnext_ref[...])
    o_ref[...] = _merge_and_ffn(x_ref[...], xn_cur, cact_cur, gy_ref[...], wgate_ref, wglu_ref,
                                wpw_ref, wout_ref, gffn_ref, wffi_ref, wffo_ref, gfin_ref, final,
                                side_work=tiles)
    cact_next, last_rows = tail()
    cact_scr[1 - cur] = cact_next
    hist_ref[...] = last_rows


def _tok_weight_specs(l):
    lc = functools.partial(_layer_const, l, pipeline_mode=pl.Buffered(1))
    return dict(
        gmix=lc((1, D_MODEL)), wgate=lc((D_MODEL, 2 * D_MODEL)),
        wglu=lc((D_SSM, 2 * D_MODEL)), wpw=lc((D_CONV, D_MODEL)),
        wout=lc((D_MODEL, D_MODEL)), gffn=lc((1, D_MODEL)), wffi=lc((D_MODEL, 2 * D_FF)),
        wffo=lc((D_FF, D_MODEL)),
        gfin=pl.BlockSpec((1, D_MODEL), lambda *_: (0, 0), pipeline_mode=pl.Buffered(1)))


def _tok_prompt(x, gy, nb, l, w, w_uc, final):
    total = x.shape[0]
    tm = TOK_TM
    hist_rows = HIST * nb
    assert tm >= hist_rows and tm % CONV_TS == 0 and total % tm == 0
    ws = _tok_weight_specs(l)
    lc = functools.partial(_layer_const, l, pipeline_mode=pl.Buffered(1))
    n_tiles = total // tm
    tile = lambda width: pl.BlockSpec((tm, width), lambda n: (n, 0))
    tile_next = pl.BlockSpec((tm, D_MODEL), lambda n: (jnp.minimum(n + 1, n_tiles - 1), 0))
    return pl.pallas_call(
        functools.partial(_tok_prompt_kernel, tm=tm, nb=nb, final=final),
        grid=(n_tiles,),
        in_specs=[tile(D_MODEL), tile_next, tile(D_SSM), ws['gmix'],
                  lc((D_MODEL, D_CONV), index=(0, D_SSM // D_CONV)),
                  lc((D_MODEL, D_CONV), index=(0, D_SSM // D_CONV + 1)),
                  lc((CONV_W, D_CONV)), lc((1, D_CONV)), lc((1, D_CONV)), lc((1, D_CONV)),
                  ws['wgate'], ws['wglu'], ws['wpw'], ws['wout'], ws['gffn'], ws['wffi'], ws['wffo'],
                  ws['gfin']],
        out_specs=[tile(D_MODEL), pl.BlockSpec((hist_rows, D_CONV), lambda n: (0, 0))],
        out_shape=[jax.ShapeDtypeStruct((total, D_MODEL), _F32),
                   jax.ShapeDtypeStruct((hist_rows, D_CONV), _F32)],
        scratch_shapes=[pltpu.VMEM((hist_rows + tm, D_CONV), _F32),
                        pltpu.VMEM((tm, D_CONV), _F32),
                        pltpu.VMEM((2, tm, D_MODEL), _BF),
                        pltpu.VMEM((2, tm, D_CONV), _BF)],
        compiler_params=pltpu.CompilerParams(
            dimension_semantics=("arbitrary",), vmem_limit_bytes=VMEM_LIMIT),
        name=f"tok_prompt_l{l}",
    )(x, x, gy, w['g_mix'], w_uc, w_uc, w['conv_w'], w['conv_b'], w['ln_g'], w['ln_b'],
      w['w_gate'], w['w_glu'], w['w_pw'], w['w_out'], w['g_ffn'], w['w_ff_in'], w['w_ff_out'], w['g_final'])


def _tok_sample_kernel(x_ref, gy_ref, cact_ref, gmix_ref, wgate_ref, wglu_ref, wpw_ref, wout_ref,
                       gffn_ref, wffi_ref, wffo_ref, gfin_ref, o_ref, *, final):
    x = x_ref[...]
    xn = _rms(x, gmix_ref[...]).astype(_BF)
    o_ref[...] = _merge_and_ffn(x, xn, cact_ref[...], gy_ref[...], wgate_ref, wglu_ref, wpw_ref,
                                wout_ref, gffn_ref, wffi_ref, wffo_ref, gfin_ref, final)


def _tok_sample(x, gy, cact, l, w, final):
    nb = x.shape[0]
    ws = _tok_weight_specs(l)
    full = lambda shape: pl.BlockSpec(shape, lambda i: (0,) * len(shape))
    return pl.pallas_call(
        functools.partial(_tok_sample_kernel, final=final),
        grid=(1,),
        in_specs=[full((nb, D_MODEL)), full((nb, D_SSM)), full((nb, D_CONV)),
                  ws['gmix'], ws['wgate'], ws['wglu'], ws['wpw'], ws['wout'], ws['gffn'], ws['wffi'],
                  ws['wffo'], ws['gfin']],
        out_specs=full((nb, D_MODEL)),
        out_shape=jax.ShapeDtypeStruct((nb, D_MODEL), _F32),
        compiler_params=pltpu.CompilerParams(
            dimension_semantics=("arbitrary",), vmem_limit_bytes=VMEM_LIMIT),
        name=f"tok_sample_l{l}",
    )(x, gy, cact, w['g_mix'], w['w_gate'], w['w_glu'], w['w_pw'], w['w_out'], w['g_ffn'],
      w['w_ff_in'], w['w_ff_out'], w['g_final'])


def kernel(x_prompt, x_sample, state_ssm_re, state_ssm_im, state_conv, g_mix, w_in, ssm_a_re, ssm_a_im,
           ssm_log_dt, ssm_b_re, ssm_b_im, ssm_c_re, ssm_c_im, ssm_d, w_glu, conv_w, conv_b, conv_ln_g,
           conv_ln_b, w_pw, w_out, g_ffn, w_ff_in, w_ff_out, g_final):
    nbp = x_prompt.shape[0]
    nbs = x_sample.shape[0]
    row = lambda a: a.astype(_F32).reshape(DEPTH, 1, -1)
    s5 = _s5_params(ssm_a_re, ssm_a_im, ssm_log_dt, ssm_b_re, ssm_b_im, ssm_c_re, ssm_c_im)
    n_uc = D_SSM + 2 * D_CONV
    w_uc = w_in[:, :, :n_uc].astype(_BF)
    w = dict(
        g_mix=row(g_mix), w_gate=w_in[:, :, n_uc:].astype(_BF), w_glu=w_glu.astype(_BF),
        conv_w=conv_w.astype(_F32), conv_b=row(conv_b), ln_g=row(conv_ln_g), ln_b=row(conv_ln_b),
        w_pw=w_pw.astype(_BF), w_out=w_out.astype(_BF), g_ffn=row(g_ffn),
        w_ff_in=w_ff_in.astype(_BF), w_ff_out=w_ff_out.astype(_BF),
        g_final=g_final.astype(_F32).reshape(1, D_MODEL), ssm_d=row(ssm_d))

    seq = x_prompt.shape[1]
    xp = x_prompt.transpose(1, 0, 2).reshape(seq * nbp, D_MODEL)
    xs = x_sample.reshape(nbs, D_MODEL)
    h_re = state_ssm_re.reshape(DEPTH, nbs, N_STATE)
    h_im = state_ssm_im.reshape(DEPTH, nbs, N_STATE)
    buf = state_conv.reshape(DEPTH, -1, LANES)

    re_p, im_p, conv_p, re_s, im_s, conv_s = [], [], [], [], [], []
    for l in range(DEPTH):
        final = l == DEPTH - 1
        gy, hre, him = _seq_prompt(xp, nbp, l, w['g_mix'], w_uc, s5, w['ssm_d'])
        xp, hist = _tok_prompt(xp, gy, nbp, l, w, w_uc, final)
        re_p.append(hre.reshape(nbp, N_GROUPS, P_STATE))
        im_p.append(him.reshape(nbp, N_GROUPS, P_STATE))
        conv_p.append(hist.reshape(HIST, nbp, D_CONV)[HIST_PAD:].transpose(1, 0, 2))

        gys, cacts, nre, nim, nbuf = _seq_sample(xs, h_re, h_im, buf, l, w, w_uc, s5)
        xs = _tok_sample(xs, gys, cacts, l, w, final)
        re_s.append(nre.reshape(nbs, N_GROUPS, P_STATE))
        im_s.append(nim.reshape(nbs, N_GROUPS, P_STATE))
        conv_s.append(nbuf.reshape(nbs, CONV_W - 1, D_CONV))

    y_prompt = xp.reshape(seq, nbp, D_MODEL).transpose(1, 0, 2)
    return (y_prompt, xs.reshape(nbs, 1, D_MODEL), jnp.stack(re_p), jnp.stack(im_p), jnp.stack(conv_p),
            jnp.stack(re_s), jnp.stack(im_s), jnp.stack(conv_s))
```

```python
import functools

import numpy as np
import jax
import jax.numpy as jnp
from jax import lax
from jax.experimental import pallas as pl
from jax.experimental.pallas import tpu as pltpu

D_MODEL = 1024
DEPTH = 4
D_SSM = 512
GROUP = 16
N_GROUPS = 32
P_STATE = 64
D_CONV = 512
CONV_W = 31
D_FF = 2816
EPS = 1e-6

LANES = 128
SUBLANES = 8
MXU = 256
LC = 16
SLOTS = LANES // GROUP
STATE_W = 2 * P_STATE
N_STATE = N_GROUPS * P_STATE
HIST = 32
HIST_PAD = HIST - (CONV_W - 1)
FF_CHUNK = MXU
VMEM_LIMIT = 58 * 1024 * 1024

SEQ_TT = 256
SEQ_RB = 32
TOK_TM = 512
CONV_TS = 64

_BF = jnp.bfloat16
_F32 = jnp.float32


def _dot(a, b):
    return jnp.dot(a, b, preferred_element_type=_F32)


def _rms(xf, g):
    return xf * lax.rsqrt(jnp.mean(xf * xf, axis=-1, keepdims=True) + EPS) * g


def _sigmoid(x):
    return jax.nn.sigmoid(x)


def _layernorm_silu(c, g, b):
    mu = jnp.mean(c, axis=-1, keepdims=True)
    var = jnp.mean(jnp.square(c - mu), axis=-1, keepdims=True)
    y = (c - mu) * lax.rsqrt(var + EPS) * g + b
    return y * _sigmoid(y)


N_POW = LC + 1
ROW_PW1 = 0
ROW_PW2 = N_POW
ROW_Q1 = 2 * N_POW
ROW_Q2 = 2 * N_POW + 1
TAB_ROWS = 40


def _swap(v):
    return pltpu.roll(v, P_STATE, 1)


def _s5_ops_kernel(tab_ref, bt_ref, cx_ref, toe_ref, wsin_ref, wsout_ref):
    lane = lax.broadcasted_iota(jnp.int32, (GROUP, LANES), 1)
    zeros = jnp.zeros((GROUP, LANES), _F32)

    def one_group(g, g8):
        row = lambda r: tab_ref[g, pl.ds(r, 1), :]
        bt0 = bt_ref[g]
        bbar = bt0 * row(ROW_Q1) + _swap(bt0) * row(ROW_Q2)
        bbar_s = _swap(bbar)
        cx = cx_ref[g]
        cxs = _swap(cx)
        ca = [cx * row(ROW_PW1 + n) - cxs * row(ROW_PW2 + n) for n in range(N_POW)]
        kt = lax.dot_general(bbar, jnp.concatenate(ca[:LC], axis=0), (((1,), (1,)), ((), ())),
                             precision=lax.Precision.HIGHEST, preferred_element_type=_F32)
        kt0, kt1 = kt[:, :LANES], kt[:, LANES:]
        for j in range(LC):
            jh, jl = divmod(j, SLOTS)
            off = LANES * jh + GROUP * ((jl + g8) % SLOTS)
            r0 = pltpu.roll(kt0, GROUP * jl, 1) if jl else kt0
            r1 = pltpu.roll(kt1, GROUP * jl, 1) if jl else kt1
            lo = lane < GROUP * jl
            if jh == 0:
                o0 = jnp.where(lo, 0.0, r0) if jl else r0
                o1 = jnp.where(lo, r0, r1) if jl else r1
            else:
                o0 = zeros
                o1 = jnp.where(lo, 0.0, r0) if jl else r0
            if g8:
                o0 = pltpu.roll(o0, GROUP * g8, 1)
                o1 = pltpu.roll(o1, GROUP * g8, 1)
            toe_ref[g, pl.ds(off, GROUP), :] = jnp.concatenate([o0, o1], axis=1).astype(_BF)
            n = LC - 1 - j
            wsin_ref[g, pl.ds(off, GROUP), :] = (
                bbar * row(ROW_PW1 + n) + bbar_s * row(ROW_PW2 + n)).astype(_BF)
        wt = jnp.concatenate(ca[1:], axis=0).T
        w0, w1 = wt[:, :LANES], wt[:, LANES:]
        if g8:
            w0 = pltpu.roll(w0, GROUP * g8, 1)
            w1 = pltpu.roll(w1, GROUP * g8, 1)
        wsout_ref[g] = jnp.concatenate([w0, w1], axis=1).astype(_BF)

    def octet(q, carry):
        for g8 in range(SLOTS):
            one_group(q * SLOTS + g8, g8)
        return carry

    lax.fori_loop(0, N_GROUPS // SLOTS, octet, 0)


def _s5_params(a_re, a_im, log_dt, b_re, b_im, c_re, c_im):
    a_re, a_im = a_re.astype(_F32), a_im.astype(_F32)
    dt = jnp.exp(log_dt.astype(_F32))[..., None]
    lre, lim = dt * a_re, dt * a_im
    n = jnp.arange(LC + 1, dtype=_F32)[None, None, :, None]
    pmag = jnp.exp(n * lre[:, :, None, :])
    pw_re = pmag * jnp.cos(n * lim[:, :, None, :])
    pw_im = pmag * jnp.sin(n * lim[:, :, None, :])
    ab_re, ab_im = pw_re[:, :, 1], pw_im[:, :, 1]
    nr, ni = ab_re - 1.0, ab_im
    den = a_re * a_re + a_im * a_im
    q_re = (nr * a_re + ni * a_im) / den
    q_im = (ni * a_re - nr * a_im) / den
    both = lambda a, b: jnp.concatenate([a, b], axis=-1)
    pw1 = both(pw_re, pw_re)
    pw2 = both(-pw_im, pw_im)
    tab = jnp.concatenate(
        [pw1, pw2, both(q_re, q_re)[:, :, None], both(-q_im, q_im)[:, :, None],
         jnp.zeros((DEPTH, N_GROUPS, TAB_ROWS - 2 * N_POW - 2, STATE_W), _F32)], axis=2)
    bt0 = both(b_re.astype(_F32).transpose(0, 1, 3, 2), b_im.astype(_F32).transpose(0, 1, 3, 2))
    cx = both(c_re.astype(_F32), -c_im.astype(_F32))

    per_layer = lambda *shape: pl.BlockSpec((None,) + shape, lambda l: (l,) + (0,) * len(shape))
    toe, w_sin, w_sout = pl.pallas_call(
        _s5_ops_kernel,
        grid=(DEPTH,),
        in_specs=[per_layer(N_GROUPS, TAB_ROWS, STATE_W), per_layer(N_GROUPS, GROUP, STATE_W),
                  per_layer(N_GROUPS, GROUP, STATE_W)],
        out_specs=[per_layer(N_GROUPS, MXU, MXU), per_layer(N_GROUPS, MXU, STATE_W),
                   per_layer(N_GROUPS, STATE_W, MXU)],
        out_shape=[jax.ShapeDtypeStruct((DEPTH, N_GROUPS, MXU, MXU), _BF),
                   jax.ShapeDtypeStruct((DEPTH, N_GROUPS, MXU, STATE_W), _BF),
                   jax.ShapeDtypeStruct((DEPTH, N_GROUPS, STATE_W, MXU), _BF)],
        compiler_params=pltpu.CompilerParams(
            dimension_semantics=("arbitrary",), vmem_limit_bytes=VMEM_LIMIT),
        name="s5_operators",
    )(tab, bt0, cx)

    flat = lambda t: t.reshape(DEPTH, 1, N_GROUPS * STATE_W)
    mult_rows = lambda n: jnp.concatenate([flat(pw1[:, :, n]), flat(pw2[:, :, n])], axis=1)
    a_lc = jnp.stack([pw_re[:, :, LC].reshape(DEPTH, N_STATE), pw_im[:, :, LC].reshape(DEPTH, N_STATE)],
                     axis=1)
    return dict(toe=toe, w_sin=w_sin, w_sout=w_sout, a_lc=a_lc, a_1=mult_rows(1))


def _seq_prompt_kernel(x_ref, gmix_ref, wu_ref, toe_ref, wsin_ref, wsout_ref, alc_ref, d_ref,
                       gy_ref, hre_ref, him_ref,
                       u_scr, lhs_scr, e_scr, hs_scr, yg_scr, y_scr, hc_scr, *, tt):
    rows = x_ref.shape[0]
    nb = rows // tt
    m = rows // LC
    nk = tt // LC
    step_rows = lambda k, j: pl.ds((LC * k + j) * nb, nb)

    @pl.when(pl.program_id(0) == 0)
    def _():
        hc_scr[...] = jnp.zeros_like(hc_scr)

    xn = _rms(x_ref[...], gmix_ref[...]).astype(_BF)
    u_scr[...] = _dot(xn, wu_ref[...])

    slot = lax.broadcasted_iota(jnp.int32, (SEQ_RB, LANES), 1) // GROUP
    kb = SEQ_RB // nb

    for v in range(D_SSM // LANES):
        ls = pl.ds(LANES * v, LANES)
        for mb in range(m // SEQ_RB):
            mrows = pl.ds(SEQ_RB * mb, SEQ_RB)
            for hf in range(LC // SLOTS):
                rolled = []
                for jl in range(SLOTS):
                    piece = jnp.concatenate(
                        [u_scr[step_rows(kb * mb + k, SLOTS * hf + jl), ls] for k in range(kb)], axis=0)
                    rolled.append(pltpu.roll(piece, GROUP * jl, 1) if jl else piece)
                for gl in range(SLOTS):
                    acc = rolled[0]
                    for jl in range(1, SLOTS):
                        acc = jnp.where(slot == (gl + jl) % SLOTS, rolled[jl], acc)
                    lhs_scr[SLOTS * v + gl, mrows, pl.ds(LANES * hf, LANES)] = acc.astype(_BF)

    for g in range(N_GROUPS):
        e_scr[g] = _dot(lhs_scr[g], wsin_ref[g])

    lo = lax.broadcasted_iota(jnp.int32, (nb, LANES), 1) < P_STATE
    re_prev, im_prev = hc_scr[0], hc_scr[1]
    re_last, im_last = [], []
    for v in range(N_GROUPS // 2):
        sl = pl.ds(LANES * v, LANES)
        ar, ai = alc_ref[0:1, sl], alc_ref[1:2, sl]
        hr = re_prev[:, LANES * v:LANES * (v + 1)]
        hi = im_prev[:, LANES * v:LANES * (v + 1)]
        for k in range(nk):
            rws = pl.ds(nb * k, nb)
            e_even, e_odd = e_scr[2 * v, rws, :], e_scr[2 * v + 1, rws, :]
            e_re = jnp.where(lo, e_even, _swap(e_odd))
            e_im = jnp.where(lo, _swap(e_even), e_odd)
            hs_scr[2 * v, rws, :] = jnp.where(lo, hr, _swap(hi))
            hs_scr[2 * v + 1, rws, :] = jnp.where(lo, _swap(hr), hi)
            hr, hi = ar * hr - ai * hi + e_re, ar * hi + ai * hr + e_im
        re_last.append(hr)
        im_last.append(hi)
    re_last = jnp.concatenate(re_last, axis=1)
    im_last = jnp.concatenate(im_last, axis=1)
    hc_scr[0] = re_last
    hc_scr[1] = im_last
    hre_ref[...] = re_last
    him_ref[...] = im_last

    for g in range(N_GROUPS):
        yg_scr[g] = _dot(lhs_scr[g], toe_ref[g]) + _dot(hs_scr[g].astype(_BF), wsout_ref[g])

    for v in range(D_SSM // LANES):
        ls = pl.ds(LANES * v, LANES)
        dv = d_ref[:, ls]
        for mb in range(m // SEQ_RB):
            mrows = pl.ds(SEQ_RB * mb, SEQ_RB)
            for hf in range(LC // SLOTS):
                src = [yg_scr[SLOTS * v + gl, mrows, pl.ds(LANES * hf, LANES)] for gl in range(SLOTS)]
                for jl in range(SLOTS):
                    acc = src[0]
                    for gl in range(1, SLOTS):
                        acc = jnp.where(slot == (gl + jl) % SLOTS, src[gl], acc)
                    if jl:
                        acc = pltpu.roll(acc, GROUP * (SLOTS - jl), 1)
                    for k in range(kb):
                        rws = step_rows(kb * mb + k, SLOTS * hf + jl)
                        y_scr[rws, ls] = acc[nb * k:nb * (k + 1)] + dv * u_scr[rws, ls]

    gy_ref[...] = jax.nn.gelu(y_scr[...]).astype(_BF)


def _layer_const(l, shape, index=None, **kw):
    idx = (l,) + (tuple(index) if index is not None else (0,) * len(shape))
    return pl.BlockSpec((None,) + tuple(shape), lambda *_: idx, **kw)


def _seq_prompt(x, nb, l, gmix, w_uc, s5, ssm_d):
    assert nb == SUBLANES
    seq = x.shape[0] // nb
    tt = SEQ_TT
    rows = nb * tt
    m = rows // LC
    lc = functools.partial(_layer_const, l, pipeline_mode=pl.Buffered(1))
    return pl.pallas_call(
        functools.partial(_seq_prompt_kernel, tt=tt),
        grid=(seq // tt,),
        in_specs=[
            pl.BlockSpec((rows, D_MODEL), lambda i: (i, 0)),
            lc((1, D_MODEL)),
            lc((D_MODEL, D_SSM)),
            lc((N_GROUPS, MXU, MXU)),
            lc((N_GROUPS, MXU, STATE_W)),
            lc((N_GROUPS, STATE_W, MXU)),
            lc((2, N_STATE)),
            lc((1, D_SSM)),
        ],
        out_specs=[
            pl.BlockSpec((rows, D_SSM), lambda i: (i, 0)),
            pl.BlockSpec((nb, N_STATE), lambda i: (0, 0)),
            pl.BlockSpec((nb, N_STATE), lambda i: (0, 0)),
        ],
        out_shape=[
            jax.ShapeDtypeStruct((seq * nb, D_SSM), _BF),
            jax.ShapeDtypeStruct((nb, N_STATE), _F32),
            jax.ShapeDtypeStruct((nb, N_STATE), _F32),
        ],
        scratch_shapes=[
            pltpu.VMEM((rows, D_SSM), _F32),
            pltpu.VMEM((N_GROUPS, m, MXU), _BF),
            pltpu.VMEM((N_GROUPS, m, STATE_W), _F32),
            pltpu.VMEM((N_GROUPS, m, STATE_W), _F32),
            pltpu.VMEM((N_GROUPS, m, MXU), _F32),
            pltpu.VMEM((rows, D_SSM), _F32),
            pltpu.VMEM((2, nb, N_STATE), _F32),
        ],
        compiler_params=pltpu.CompilerParams(
            dimension_semantics=("arbitrary",), vmem_limit_bytes=VMEM_LIMIT),
        name=f"seq_prompt_l{l}",
    )(x, gmix, w_uc, s5['toe'], s5['w_sin'], s5['w_sout'], s5['a_lc'], ssm_d)


def _seq_sample_kernel(x_ref, hre_ref, him_ref, buf_ref, gmix_ref, wuc_ref, toe_ref, wsin_ref, wsout_ref,
                       a1_ref, d_ref, cw_ref, cb_ref, lng_ref, lnb_ref,
                       gy_ref, cact_ref, nre_ref, nim_ref, nbuf_ref):
    nb = x_ref.shape[0]
    xn = _rms(x_ref[...], gmix_ref[...]).astype(_BF)
    z = _dot(xn, wuc_ref[...])
    u = z[:, :D_SSM]
    c = z[:, D_SSM:D_SSM + D_CONV] * _sigmoid(z[:, D_SSM + D_CONV:])

    lane = lax.broadcasted_iota(jnp.int32, (nb, LANES), 1)
    slot = lane // GROUP
    lo = lane < P_STATE
    zero = jnp.zeros((nb, LANES), _F32)
    last = LC - 1

    ys = []
    for v in range(D_SSM // LANES):
        uv = u[:, LANES * v:LANES * (v + 1)]
        u_last = pltpu.roll(uv, GROUP * (last % SLOTS), 1)
        yv = None
        for gp in range(SLOTS // 2):
            sv = (SLOTS * v) // 2 + gp
            rv = hre_ref[:, pl.ds(LANES * sv, LANES)]
            iv = him_ref[:, pl.ds(LANES * sv, LANES)]
            h0s = (jnp.where(lo, rv, pltpu.roll(iv, P_STATE, 1)),
                   jnp.where(lo, pltpu.roll(rv, P_STATE, 1), iv))
            hn = []
            for par in range(2):
                gl = 2 * gp + par
                g = SLOTS * v + gl
                h0 = h0s[par]
                lhs_l = jnp.concatenate(
                    [zero, jnp.where(slot == (gl + last) % SLOTS, u_last, 0.0)], axis=1).astype(_BF)
                e = _dot(lhs_l, wsin_ref[g])
                sl = pl.ds(STATE_W * g, STATE_W)
                hn.append(h0 * a1_ref[0:1, sl] + pltpu.roll(h0, P_STATE, 1) * a1_ref[1:2, sl] + e)
                lhs_f = jnp.concatenate([jnp.where(slot == gl, uv, 0.0), zero], axis=1).astype(_BF)
                yg = (_dot(lhs_f, toe_ref[g, :, pl.ds(0, LANES)])
                      + _dot(h0.astype(_BF), wsout_ref[g, :, pl.ds(0, LANES)]))
                yv = yg if yv is None else jnp.where(slot == gl, yg, yv)
            nre_ref[:, pl.ds(LANES * sv, LANES)] = jnp.where(lo, hn[0], pltpu.roll(hn[1], P_STATE, 1))
            nim_ref[:, pl.ds(LANES * sv, LANES)] = jnp.where(lo, pltpu.roll(hn[0], P_STATE, 1), hn[1])
        ys.append(yv + d_ref[:, pl.ds(LANES * v, LANES)] * uv)
    gy_ref[...] = jax.nn.gelu(jnp.concatenate(ys, axis=1)).astype(_BF)

    acc = cb_ref[...] + cw_ref[pl.ds(CONV_W - 1, 1), :] * c
    for k in range(CONV_W - 1):
        tap = buf_ref[:, k, :]
        acc = acc + cw_ref[pl.ds(k, 1), :] * tap
        if k:
            nbuf_ref[:, k - 1, :] = tap
    nbuf_ref[:, CONV_W - 2, :] = c
    cact_ref[...] = _layernorm_silu(acc, lng_ref[...], lnb_ref[...]).astype(_BF)


def _seq_sample(x, h_re, h_im, buf, l, w, w_uc, s5):
    nb = x.shape[0]
    hist = (nb, CONV_W - 1, D_CONV)
    one = pl.Buffered(1)
    full = lambda shape: pl.BlockSpec(shape, lambda i: (0,) * len(shape))
    lc = functools.partial(_layer_const, l, pipeline_mode=one)
    return pl.pallas_call(
        _seq_sample_kernel,
        grid=(1,),
        in_specs=[
            full((nb, D_MODEL)), lc((nb, N_STATE)), lc((nb, N_STATE)), lc(hist),
            lc((1, D_MODEL)), lc((D_MODEL, D_SSM + 2 * D_CONV)),
            lc((N_GROUPS, MXU, MXU)), lc((N_GROUPS, MXU, STATE_W)), lc((N_GROUPS, STATE_W, MXU)),
            lc((2, N_GROUPS * STATE_W)), lc((1, D_SSM)),
            lc((CONV_W, D_CONV)), lc((1, D_CONV)), lc((1, D_CONV)), lc((1, D_CONV)),
        ],
        out_specs=[full((nb, D_SSM)), full((nb, D_CONV)), full((nb, N_STATE)), full((nb, N_STATE)),
                   full(hist)],
        out_shape=[
            jax.ShapeDtypeStruct((nb, D_SSM), _BF),
            jax.ShapeDtypeStruct((nb, D_CONV), _BF),
            jax.ShapeDtypeStruct((nb, N_STATE), _F32),
            jax.ShapeDtypeStruct((nb, N_STATE), _F32),
            jax.ShapeDtypeStruct(hist, _F32),
        ],
        compiler_params=pltpu.CompilerParams(
            dimension_semantics=("arbitrary",), vmem_limit_bytes=VMEM_LIMIT),
        name=f"seq_sample_l{l}",
    )(x, h_re, h_im, buf, w['g_mix'], w_uc, s5['toe'], s5['w_sin'], s5['w_sout'], s5['a_1'], w['ssm_d'],
      w['conv_w'], w['conv_b'], w['ln_g'], w['ln_b'])


def _merge_and_ffn(x, xn, cact, gy, wgate_ref, wglu_ref, wpw_ref, wout_ref, gffn_ref,
                   wffi_ref, wffo_ref, gfin_ref, final, side_work=()):
    yg = _dot(gy, wglu_ref[...])
    ya = yg[:, :D_MODEL] * _sigmoid(yg[:, D_MODEL:])
    yb = _dot(cact, wpw_ref[...])
    ga = _sigmoid(_dot(xn, wgate_ref[:, pl.ds(0, D_MODEL)]))
    gb = _sigmoid(_dot(xn, wgate_ref[:, pl.ds(D_MODEL, D_MODEL)]))
    merged = (ga * ya + gb * yb).astype(_BF)
    x1 = x + _dot(merged, wout_ref[...])
    hn = _rms(x1, gffn_ref[...]).astype(_BF)
    acc = x1
    n_chunks = D_FF // FF_CHUNK
    per_chunk = -(-len(side_work) // n_chunks)
    for ck in range(n_chunks):
        h1 = _dot(hn, wffi_ref[:, pl.ds(FF_CHUNK * ck, FF_CHUNK)])
        h2 = _dot(hn, wffi_ref[:, pl.ds(D_FF + FF_CHUNK * ck, FF_CHUNK)])
        a = (h1 * _sigmoid(h1) * h2).astype(_BF)
        acc = acc + _dot(a, wffo_ref[pl.ds(FF_CHUNK * ck, FF_CHUNK), :])
        for thunk in side_work[per_chunk * ck:per_chunk * (ck + 1)]:
            thunk()
    if final:
        acc = _rms(acc, gfin_ref[...])
    return acc


def _conv_head(x, gmix_ref, wcv_ref, wcg_ref, cfull_scr, tm, hist_rows):
    xn = _rms(x, gmix_ref[...]).astype(_BF)
    cfull_scr[pl.ds(hist_rows, tm), :] = _dot(xn, wcv_ref[...]) * _sigmoid(_dot(xn, wcg_ref[...]))
    return xn


def _conv_tiles(cw_ref, cb_ref, cfull_scr, conv_scr, tm, nb):
    def tile(t0, lb):
        ls = pl.ds(LANES * lb, LANES)
        acc = jnp.broadcast_to(cb_ref[:, ls], (CONV_TS, LANES))
        for k in range(CONV_W):
            acc = acc + cw_ref[pl.ds(k, 1), ls] * cfull_scr[pl.ds(t0 + nb * (k + HIST_PAD), CONV_TS), ls]
        conv_scr[pl.ds(t0, CONV_TS), ls] = acc
    return [functools.partial(tile, ts * CONV_TS, lb)
            for ts in range(tm // CONV_TS) for lb in range(D_CONV // LANES)]


def _conv_tail(lng_ref, lnb_ref, cfull_scr, conv_scr, tm, hist_rows):
    cact = _layernorm_silu(conv_scr[...], lng_ref[...], lnb_ref[...]).astype(_BF)
    tail = cfull_scr[pl.ds(tm, hist_rows), :]
    cfull_scr[pl.ds(0, hist_rows), :] = tail
    return cact, tail


def _tok_prompt_kernel(x_ref, xnext_ref, gy_ref, gmix_ref, wcv_ref, wcg_ref, cw_ref, cb_ref, lng_ref,
                       lnb_ref, wgate_ref, wglu_ref, wpw_ref, wout_ref, gffn_ref, wffi_ref, wffo_ref,
                       gfin_ref, o_ref, hist_ref, cfull_scr, conv_scr, xn_scr, cact_scr,
                       *, tm, nb, final):
    n = pl.program_id(0)
    hist_rows = HIST * nb
    head = functools.partial(_conv_head, gmix_ref=gmix_ref, wcv_ref=wcv_ref, wcg_ref=wcg_ref,
                             cfull_scr=cfull_scr, tm=tm, hist_rows=hist_rows)
    tiles = _conv_tiles(cw_ref, cb_ref, cfull_scr, conv_scr, tm, nb)
    tail = functools.partial(_conv_tail, lng_ref, lnb_ref, cfull_scr, conv_scr, tm, hist_rows)

    @pl.when(n == 0)
    def _():
        cfull_scr[pl.ds(0, hist_rows), :] = jnp.zeros((hist_rows, D_CONV), _F32)
        xn_scr[0] = head(x_ref[...])
        for thunk in tiles:
            thunk()
        cact_scr[0] = tail()[0]

    cur = n % 2
    xn_cur = xn_scr[cur]
    cact_cur = cact_scr[cur]
    xn_scr[1 - cur] = head(xnext_ref[...])
    o_ref[...] = _merge_and_ffn(x_ref[...], xn_cur, cact_cur, gy_ref[...], wgate_ref, wglu_ref,
                                wpw_ref, wout_ref, gffn_ref, wffi_ref, wffo_ref, gfin_ref, final,
                                side_work=tiles)
    cact_next, last_rows = tail()
    cact_scr[1 - cur] = cact_next
    hist_ref[...] = last_rows


def _tok_weight_specs(l):
    lc = functools.partial(_layer_const, l, pipeline_mode=pl.Buffered(1))
    return dict(
        gmix=lc((1, D_MODEL)), wgate=lc((D_MODEL, 2 * D_MODEL)),
        wglu=lc((D_SSM, 2 * D_MODEL)), wpw=lc((D_CONV, D_MODEL)),
        wout=lc((D_MODEL, D_MODEL)), gffn=lc((1, D_MODEL)), wffi=lc((D_MODEL, 2 * D_FF)),
        wffo=lc((D_FF, D_MODEL)),
        gfin=pl.BlockSpec((1, D_MODEL), lambda *_: (0, 0), pipeline_mode=pl.Buffered(1)))


def _tok_prompt(x, gy, nb, l, w, w_uc, final):
    total = x.shape[0]
    tm = TOK_TM
    hist_rows = HIST * nb
    assert tm >= hist_rows and tm % CONV_TS == 0 and total % tm == 0
    ws = _tok_weight_specs(l)
    lc = functools.partial(_layer_const, l, pipeline_mode=pl.Buffered(1))
    n_tiles = total // tm
    tile = lambda width: pl.BlockSpec((tm, width), lambda n: (n, 0))
    tile_next = pl.BlockSpec((tm, D_MODEL), lambda n: (jnp.minimum(n + 1, n_tiles - 1), 0))
    return pl.pallas_call(
        functools.partial(_tok_prompt_kernel, tm=tm, nb=nb, final=final),
        grid=(n_tiles,),
        in_specs=[tile(D_MODEL), tile_next, tile(D_SSM), ws['gmix'],
                  lc((D_MODEL, D_CONV), index=(0, D_SSM // D_CONV)),
                  lc((D_MODEL, D_CONV), index=(0, D_SSM // D_CONV + 1)),
                  lc((CONV_W, D_CONV)), lc((1, D_CONV)), lc((1, D_CONV)), lc((1, D_CONV)),
                  ws['wgate'], ws['wglu'], ws['wpw'], ws['wout'], ws['gffn'], ws['wffi'], ws['wffo'],
                  ws['gfin']],
        out_specs=[tile(D_MODEL), pl.BlockSpec((hist_rows, D_CONV), lambda n: (0, 0))],
        out_shape=[jax.ShapeDtypeStruct((total, D_MODEL), _F32),
                   jax.ShapeDtypeStruct((hist_rows, D_CONV), _F32)],
        scratch_shapes=[pltpu.VMEM((hist_rows + tm, D_CONV), _F32),
                        pltpu.VMEM((tm, D_CONV), _F32),
                        pltpu.VMEM((2, tm, D_MODEL), _BF),
                        pltpu.VMEM((2, tm, D_CONV), _BF)],
        compiler_params=pltpu.CompilerParams(
            dimension_semantics=("arbitrary",), vmem_limit_bytes=VMEM_LIMIT),
        name=f"tok_prompt_l{l}",
    )(x, x, gy, w['g_mix'], w_uc, w_uc, w['conv_w'], w['conv_b'], w['ln_g'], w['ln_b'],
      w['w_gate'], w['w_glu'], w['w_pw'], w['w_out'], w['g_ffn'], w['w_ff_in'], w['w_ff_out'], w['g_final'])


def _tok_sample_kernel(x_ref, gy_ref, cact_ref, gmix_ref, wgate_ref, wglu_ref, wpw_ref, wout_ref,
                       gffn_ref, wffi_ref, wffo_ref, gfin_ref, o_ref, *, final):
    x = x_ref[...]
    xn = _rms(x, gmix_ref[...]).astype(_BF)
    o_ref[...] = _merge_and_ffn(x, xn, cact_ref[...], gy_ref[...], wgate_ref, wglu_ref, wpw_ref,
                                wout_ref, gffn_ref, wffi_ref, wffo_ref, gfin_ref, final)


def _tok_sample(x, gy, cact, l, w, final):
    nb = x.shape[0]
    ws = _tok_weight_specs(l)
    full = lambda shape: pl.BlockSpec(shape, lambda i: (0,) * len(shape))
    return pl.pallas_call(
        functools.partial(_tok_sample_kernel, final=final),
        grid=(1,),
        in_specs=[full((nb, D_MODEL)), full((nb, D_SSM)), full((nb, D_CONV)),
                  ws['gmix'], ws['wgate'], ws['wglu'], ws['wpw'], ws['wout'], ws['gffn'], ws['wffi'],
                  ws['wffo'], ws['gfin']],
        out_specs=full((nb, D_MODEL)),
        out_shape=jax.ShapeDtypeStruct((nb, D_MODEL), _F32),
        compiler_params=pltpu.CompilerParams(
            dimension_semantics=("arbitrary",), vmem_limit_bytes=VMEM_LIMIT),
        name=f"tok_sample_l{l}",
    )(x, gy, cact, w['g_mix'], w['w_gate'], w['w_glu'], w['w_pw'], w['w_out'], w['g_ffn'],
      w['w_ff_in'], w['w_ff_out'], w['g_final'])


def kernel(x_prompt, x_sample, state_ssm_re, state_ssm_im, state_conv, g_mix, w_in, ssm_a_re, ssm_a_im,
           ssm_log_dt, ssm_b_re, ssm_b_im, ssm_c_re, ssm_c_im, ssm_d, w_glu, conv_w, conv_b, conv_ln_g,
           conv_ln_b, w_pw, w_out, g_ffn, w_ff_in, w_ff_out, g_final):
    nbp = x_prompt.shape[0]
    nbs = x_sample.shape[0]
    row = lambda a: a.astype(_F32).reshape(DEPTH, 1, -1)
    s5 = _s5_params(ssm_a_re, ssm_a_im, ssm_log_dt, ssm_b_re, ssm_b_im, ssm_c_re, ssm_c_im)
    n_uc = D_SSM + 2 * D_CONV
    w_uc = w_in[:, :, :n_uc].astype(_BF)
    w = dict(
        g_mix=row(g_mix), w_gate=w_in[:, :, n_uc:].astype(_BF), w_glu=w_glu.astype(_BF),
        conv_w=conv_w.astype(_F32), conv_b=row(conv_b), ln_g=row(conv_ln_g), ln_b=row(conv_ln_b),
        w_pw=w_pw.astype(_BF), w_out=w_out.astype(_BF), g_ffn=row(g_ffn),
        w_ff_in=w_ff_in.astype(_BF), w_ff_out=w_ff_out.astype(_BF),
        g_final=g_final.astype(_F32).reshape(1, D_MODEL), ssm_d=row(ssm_d))

    seq = x_prompt.shape[1]
    xp = x_prompt.transpose(1, 0, 2).reshape(seq * nbp, D_MODEL)
    xs = x_sample.reshape(nbs, D_MODEL)
    h_re = state_ssm_re.reshape(DEPTH, nbs, N_STATE)
    h_im = state_ssm_im.reshape(DEPTH, nbs, N_STATE)
    buf = state_conv

    re_p, im_p, conv_p, re_s, im_s, conv_s = [], [], [], [], [], []
    for l in range(DEPTH):
        final = l == DEPTH - 1
        gy, hre, him = _seq_prompt(xp, nbp, l, w['g_mix'], w_uc, s5, w['ssm_d'])
        xp, hist = _tok_prompt(xp, gy, nbp, l, w, w_uc, final)
        re_p.append(hre.reshape(nbp, N_GROUPS, P_STATE))
        im_p.append(him.reshape(nbp, N_GROUPS, P_STATE))
        conv_p.append(hist.reshape(HIST, nbp, D_CONV)[HIST_PAD:].transpose(1, 0, 2))

        gys, cacts, nre, nim, nbuf = _seq_sample(xs, h_re, h_im, buf, l, w, w_uc, s5)
        xs = _tok_sample(xs, gys, cacts, l, w, final)
        re_s.append(nre.reshape(nbs, N_GROUPS, P_STATE))
        im_s.append(nim.reshape(nbs, N_GROUPS, P_STATE))
        conv_s.append(nbuf)

    y_prompt = xp.reshape(seq, nbp, D_MODEL).transpose(1, 0, 2)
    return (y_prompt, xs.reshape(nbs, 1, D_MODEL), jnp.stack(re_p), jnp.stack(im_p), jnp.stack(conv_p),
            jnp.stack(re_s), jnp.stack(im_s), jnp.stack(conv_s))
```

```python
import functools

import numpy as np
import jax
import jax.numpy as jnp
from jax import lax
from jax.experimental import pallas as pl
from jax.experimental.pallas import tpu as pltpu

D_MODEL = 1024
DEPTH = 4
D_SSM = 512
GROUP = 16
N_GROUPS = 32
P_STATE = 64
D_CONV = 512
CONV_W = 31
D_FF = 2816
EPS = 1e-6

LANES = 128
SUBLANES = 8
MXU = 256
LC = 16
SLOTS = LANES // GROUP
STATE_W = 2 * P_STATE
N_STATE = N_GROUPS * P_STATE
HIST = 32
HIST_PAD = HIST - (CONV_W - 1)
FF_CHUNK = MXU
VMEM_LIMIT = 58 * 1024 * 1024

SEQ_TT = 256
SEQ_RB = 32
TOK_TM = 512
CONV_TS = 64

_BF = jnp.bfloat16
_F32 = jnp.float32


def _dot(a, b):
    return jnp.dot(a, b, preferred_element_type=_F32)


def _rms(xf, g):
    return xf * lax.rsqrt(jnp.mean(xf * xf, axis=-1, keepdims=True) + EPS) * g


def _sigmoid(x):
    return jax.nn.sigmoid(x)


def _layernorm_silu(c, g, b):
    mu = jnp.mean(c, axis=-1, keepdims=True)
    var = jnp.mean(jnp.square(c - mu), axis=-1, keepdims=True)
    y = (c - mu) * lax.rsqrt(var + EPS) * g + b
    return y * _sigmoid(y)


N_POW = LC + 1
ROW_PW1 = 0
ROW_PW2 = N_POW
ROW_Q1 = 2 * N_POW
ROW_Q2 = 2 * N_POW + 1
TAB_ROWS = 40


def _swap(v):
    return pltpu.roll(v, P_STATE, 1)


def _s5_ops_kernel(tab_ref, bt_ref, cx_ref, toe_ref, wsin_ref, wsout_ref):
    lane = lax.broadcasted_iota(jnp.int32, (GROUP, LANES), 1)
    zeros = jnp.zeros((GROUP, LANES), _F32)

    def one_group(g, g8):
        row = lambda r: tab_ref[g, pl.ds(r, 1), :]
        bt0 = bt_ref[g]
        bbar = bt0 * row(ROW_Q1) + _swap(bt0) * row(ROW_Q2)
        bbar_s = _swap(bbar)
        cx = cx_ref[g]
        cxs = _swap(cx)
        ca = [cx * row(ROW_PW1 + n) - cxs * row(ROW_PW2 + n) for n in range(N_POW)]
        kt = lax.dot_general(bbar, jnp.concatenate(ca[:LC], axis=0), (((1,), (1,)), ((), ())),
                             precision=lax.Precision.HIGHEST, preferred_element_type=_F32)
        kt0, kt1 = kt[:, :LANES], kt[:, LANES:]
        for j in range(LC):
            jh, jl = divmod(j, SLOTS)
            off = LANES * jh + GROUP * ((jl + g8) % SLOTS)
            r0 = pltpu.roll(kt0, GROUP * jl, 1) if jl else kt0
            r1 = pltpu.roll(kt1, GROUP * jl, 1) if jl else kt1
            lo = lane < GROUP * jl
            if jh == 0:
                o0 = jnp.where(lo, 0.0, r0) if jl else r0
                o1 = jnp.where(lo, r0, r1) if jl else r1
            else:
                o0 = zeros
                o1 = jnp.where(lo, 0.0, r0) if jl else r0
            if g8:
                o0 = pltpu.roll(o0, GROUP * g8, 1)
                o1 = pltpu.roll(o1, GROUP * g8, 1)
            toe_ref[g, pl.ds(off, GROUP), :] = jnp.concatenate([o0, o1], axis=1).astype(_BF)
            n = LC - 1 - j
            wsin_ref[g, pl.ds(off, GROUP), :] = (
                bbar * row(ROW_PW1 + n) + bbar_s * row(ROW_PW2 + n)).astype(_BF)
        wt = jnp.concatenate(ca[1:], axis=0).T
        w0, w1 = wt[:, :LANES], wt[:, LANES:]
        if g8:
            w0 = pltpu.roll(w0, GROUP * g8, 1)
            w1 = pltpu.roll(w1, GROUP * g8, 1)
        wsout_ref[g] = jnp.concatenate([w0, w1], axis=1).astype(_BF)

    def octet(q, carry):
        for g8 in range(SLOTS):
            one_group(q * SLOTS + g8, g8)
        return carry

    lax.fori_loop(0, N_GROUPS // SLOTS, octet, 0)


def _s5_params(a_re, a_im, log_dt, b_re, b_im, c_re, c_im):
    a_re, a_im = a_re.astype(_F32), a_im.astype(_F32)
    dt = jnp.exp(log_dt.astype(_F32))[..., None]
    lre, lim = dt * a_re, dt * a_im
    n = jnp.arange(LC + 1, dtype=_F32)[None, None, :, None]
    pmag = jnp.exp(n * lre[:, :, None, :])
    pw_re = pmag * jnp.cos(n * lim[:, :, None, :])
    pw_im = pmag * jnp.sin(n * lim[:, :, None, :])
    ab_re, ab_im = pw_re[:, :, 1], pw_im[:, :, 1]
    nr, ni = ab_re - 1.0, ab_im
    den = a_re * a_re + a_im * a_im
    q_re = (nr * a_re + ni * a_im) / den
    q_im = (ni * a_re - nr * a_im) / den
    both = lambda a, b: jnp.concatenate([a, b], axis=-1)
    pw1 = both(pw_re, pw_re)
    pw2 = both(-pw_im, pw_im)
    tab = jnp.concatenate(
        [pw1, pw2, both(q_re, q_re)[:, :, None], both(-q_im, q_im)[:, :, None],
         jnp.zeros((DEPTH, N_GROUPS, TAB_ROWS - 2 * N_POW - 2, STATE_W), _F32)], axis=2)
    bt0 = both(b_re.astype(_F32).transpose(0, 1, 3, 2), b_im.astype(_F32).transpose(0, 1, 3, 2))
    cx = both(c_re.astype(_F32), -c_im.astype(_F32))

    per_layer = lambda *shape: pl.BlockSpec((None,) + shape, lambda l: (l,) + (0,) * len(shape))
    toe, w_sin, w_sout = pl.pallas_call(
        _s5_ops_kernel,
        grid=(DEPTH,),
        in_specs=[per_layer(N_GROUPS, TAB_ROWS, STATE_W), per_layer(N_GROUPS, GROUP, STATE_W),
                  per_layer(N_GROUPS, GROUP, STATE_W)],
        out_specs=[per_layer(N_GROUPS, MXU, MXU), per_layer(N_GROUPS, MXU, STATE_W),
                   per_layer(N_GROUPS, STATE_W, MXU)],
        out_shape=[jax.ShapeDtypeStruct((DEPTH, N_GROUPS, MXU, MXU), _BF),
                   jax.ShapeDtypeStruct((DEPTH, N_GROUPS, MXU, STATE_W), _BF),
                   jax.ShapeDtypeStruct((DEPTH, N_GROUPS, STATE_W, MXU), _BF)],
        compiler_params=pltpu.CompilerParams(
            dimension_semantics=("arbitrary",), vmem_limit_bytes=VMEM_LIMIT),
        name="s5_operators",
    )(tab, bt0, cx)

    flat = lambda t: t.reshape(DEPTH, 1, N_GROUPS * STATE_W)
    mult_rows = lambda n: jnp.concatenate([flat(pw1[:, :, n]), flat(pw2[:, :, n])], axis=1)
    a_lc = jnp.stack([pw_re[:, :, LC].reshape(DEPTH, N_STATE), pw_im[:, :, LC].reshape(DEPTH, N_STATE)],
                     axis=1)
    return dict(toe=toe, w_sin=w_sin, w_sout=w_sout, a_lc=a_lc, a_1=mult_rows(1))


def _seq_prompt_kernel(x_ref, gmix_ref, wu_ref, toe_ref, wsin_ref, wsout_ref, alc_ref, d_ref,
                       gy_ref, hre_ref, him_ref,
                       u_scr, lhs_scr, e_scr, hs_scr, yg_scr, y_scr, hc_scr, *, tt):
    rows = x_ref.shape[0]
    nb = rows // tt
    m = rows // LC
    nk = tt // LC
    step_rows = lambda k, j: pl.ds((LC * k + j) * nb, nb)

    @pl.when(pl.program_id(0) == 0)
    def _():
        hc_scr[...] = jnp.zeros_like(hc_scr)

    xn = _rms(x_ref[...], gmix_ref[...]).astype(_BF)
    u_scr[...] = _dot(xn, wu_ref[...])

    slot = lax.broadcasted_iota(jnp.int32, (SEQ_RB, LANES), 1) // GROUP
    kb = SEQ_RB // nb

    for v in range(D_SSM // LANES):
        ls = pl.ds(LANES * v, LANES)
        for mb in range(m // SEQ_RB):
            mrows = pl.ds(SEQ_RB * mb, SEQ_RB)
            for hf in range(LC // SLOTS):
                rolled = []
                for jl in range(SLOTS):
                    piece = jnp.concatenate(
                        [u_scr[step_rows(kb * mb + k, SLOTS * hf + jl), ls] for k in range(kb)], axis=0)
                    rolled.append(pltpu.roll(piece, GROUP * jl, 1) if jl else piece)
                for gl in range(SLOTS):
                    acc = rolled[0]
                    for jl in range(1, SLOTS):
                        acc = jnp.where(slot == (gl + jl) % SLOTS, rolled[jl], acc)
                    lhs_scr[SLOTS * v + gl, mrows, pl.ds(LANES * hf, LANES)] = acc.astype(_BF)

    for g in range(N_GROUPS):
        e_scr[g] = _dot(lhs_scr[g], wsin_ref[g])

    lo = lax.broadcasted_iota(jnp.int32, (nb, LANES), 1) < P_STATE
    re_prev, im_prev = hc_scr[0], hc_scr[1]
    re_last, im_last = [], []
    for v in range(N_GROUPS // 2):
        sl = pl.ds(LANES * v, LANES)
        ar, ai = alc_ref[0:1, sl], alc_ref[1:2, sl]
        hr = re_prev[:, LANES * v:LANES * (v + 1)]
        hi = im_prev[:, LANES * v:LANES * (v + 1)]
        for k in range(nk):
            rws = pl.ds(nb * k, nb)
            e_even, e_odd = e_scr[2 * v, rws, :], e_scr[2 * v + 1, rws, :]
            e_re = jnp.where(lo, e_even, _swap(e_odd))
            e_im = jnp.where(lo, _swap(e_even), e_odd)
            hs_scr[2 * v, rws, :] = jnp.where(lo, hr, _swap(hi))
            hs_scr[2 * v + 1, rws, :] = jnp.where(lo, _swap(hr), hi)
            hr, hi = ar * hr - ai * hi + e_re, ar * hi + ai * hr + e_im
        re_last.append(hr)
        im_last.append(hi)
    re_last = jnp.concatenate(re_last, axis=1)
    im_last = jnp.concatenate(im_last, axis=1)
    hc_scr[0] = re_last
    hc_scr[1] = im_last
    hre_ref[...] = re_last
    him_ref[...] = im_last

    for g in range(N_GROUPS):
        yg_scr[g] = _dot(lhs_scr[g], toe_ref[g]) + _dot(hs_scr[g].astype(_BF), wsout_ref[g])

    for v in range(D_SSM // LANES):
        ls = pl.ds(LANES * v, LANES)
        dv = d_ref[:, ls]
        for mb in range(m // SEQ_RB):
            mrows = pl.ds(SEQ_RB * mb, SEQ_RB)
            for hf in range(LC // SLOTS):
                src = [yg_scr[SLOTS * v + gl, mrows, pl.ds(LANES * hf, LANES)] for gl in range(SLOTS)]
                for jl in range(SLOTS):
                    acc = src[0]
                    for gl in range(1, SLOTS):
                        acc = jnp.where(slot == (gl + jl) % SLOTS, src[gl], acc)
                    if jl:
                        acc = pltpu.roll(acc, GROUP * (SLOTS - jl), 1)
                    for k in range(kb):
                        rws = step_rows(kb * mb + k, SLOTS * hf + jl)
                        y_scr[rws, ls] = acc[nb * k:nb * (k + 1)] + dv * u_scr[rws, ls]

    gy_ref[...] = jax.nn.gelu(y_scr[...]).astype(_BF)


def _layer_const(l, shape, index=None, **kw):
    idx = (l,) + (tuple(index) if index is not None else (0,) * len(shape))
    return pl.BlockSpec((None,) + tuple(shape), lambda *_: idx, **kw)


def _seq_prompt(x, nb, l, gmix, w_uc, s5, ssm_d):
    assert nb == SUBLANES
    seq = x.shape[0] // nb
    tt = SEQ_TT
    rows = nb * tt
    m = rows // LC
    lc = functools.partial(_layer_const, l, pipeline_mode=pl.Buffered(1))
    return pl.pallas_call(
        functools.partial(_seq_prompt_kernel, tt=tt),
        grid=(seq // tt,),
        in_specs=[
            pl.BlockSpec((rows, D_MODEL), lambda i: (i, 0)),
            lc((1, D_MODEL)),
            lc((D_MODEL, D_SSM)),
            lc((N_GROUPS, MXU, MXU)),
            lc((N_GROUPS, MXU, STATE_W)),
            lc((N_GROUPS, STATE_W, MXU)),
            lc((2, N_STATE)),
            lc((1, D_SSM)),
        ],
        out_specs=[
            pl.BlockSpec((rows, D_SSM), lambda i: (i, 0)),
            pl.BlockSpec((nb, N_STATE), lambda i: (0, 0)),
            pl.BlockSpec((nb, N_STATE), lambda i: (0, 0)),
        ],
        out_shape=[
            jax.ShapeDtypeStruct((seq * nb, D_SSM), _BF),
            jax.ShapeDtypeStruct((nb, N_STATE), _F32),
            jax.ShapeDtypeStruct((nb, N_STATE), _F32),
        ],
        scratch_shapes=[
            pltpu.VMEM((rows, D_SSM), _F32),
            pltpu.VMEM((N_GROUPS, m, MXU), _BF),
            pltpu.VMEM((N_GROUPS, m, STATE_W), _F32),
            pltpu.VMEM((N_GROUPS, m, STATE_W), _F32),
            pltpu.VMEM((N_GROUPS, m, MXU), _F32),
            pltpu.VMEM((rows, D_SSM), _F32),
            pltpu.VMEM((2, nb, N_STATE), _F32),
        ],
        compiler_params=pltpu.CompilerParams(
            dimension_semantics=("arbitrary",), vmem_limit_bytes=VMEM_LIMIT),
        name=f"seq_prompt_l{l}",
    )(x, gmix, w_uc, s5['toe'], s5['w_sin'], s5['w_sout'], s5['a_lc'], ssm_d)


def _seq_sample_kernel(x_ref, hre_ref, him_ref, buf_ref, gmix_ref, wuc_ref, toe_ref, wsin_ref, wsout_ref,
                       a1_ref, d_ref, cw_ref, cb_ref, lng_ref, lnb_ref, nbuf_all_ref,
                       gy_ref, cact_ref, nre_ref, nim_ref, nbuf_ref):
    del nbuf_all_ref
    nb = x_ref.shape[0]
    xn = _rms(x_ref[...], gmix_ref[...]).astype(_BF)
    z = _dot(xn, wuc_ref[...])
    u = z[:, :D_SSM]
    c = z[:, D_SSM:D_SSM + D_CONV] * _sigmoid(z[:, D_SSM + D_CONV:])

    lane = lax.broadcasted_iota(jnp.int32, (nb, LANES), 1)
    slot = lane // GROUP
    lo = lane < P_STATE
    zero = jnp.zeros((nb, LANES), _F32)
    last = LC - 1

    ys = []
    for v in range(D_SSM // LANES):
        uv = u[:, LANES * v:LANES * (v + 1)]
        u_last = pltpu.roll(uv, GROUP * (last % SLOTS), 1)
        yv = None
        for gp in range(SLOTS // 2):
            sv = (SLOTS * v) // 2 + gp
            rv = hre_ref[:, pl.ds(LANES * sv, LANES)]
            iv = him_ref[:, pl.ds(LANES * sv, LANES)]
            h0s = (jnp.where(lo, rv, pltpu.roll(iv, P_STATE, 1)),
                   jnp.where(lo, pltpu.roll(rv, P_STATE, 1), iv))
            hn = []
            for par in range(2):
                gl = 2 * gp + par
                g = SLOTS * v + gl
                h0 = h0s[par]
                lhs_l = jnp.concatenate(
                    [zero, jnp.where(slot == (gl + last) % SLOTS, u_last, 0.0)], axis=1).astype(_BF)
                e = _dot(lhs_l, wsin_ref[g])
                sl = pl.ds(STATE_W * g, STATE_W)
                hn.append(h0 * a1_ref[0:1, sl] + pltpu.roll(h0, P_STATE, 1) * a1_ref[1:2, sl] + e)
                lhs_f = jnp.concatenate([jnp.where(slot == gl, uv, 0.0), zero], axis=1).astype(_BF)
                yg = (_dot(lhs_f, toe_ref[g, :, pl.ds(0, LANES)])
                      + _dot(h0.astype(_BF), wsout_ref[g, :, pl.ds(0, LANES)]))
                yv = yg if yv is None else jnp.where(slot == gl, yg, yv)
            nre_ref[:, pl.ds(LANES * sv, LANES)] = jnp.where(lo, hn[0], pltpu.roll(hn[1], P_STATE, 1))
            nim_ref[:, pl.ds(LANES * sv, LANES)] = jnp.where(lo, pltpu.roll(hn[0], P_STATE, 1), hn[1])
        ys.append(yv + d_ref[:, pl.ds(LANES * v, LANES)] * uv)
    gy_ref[...] = jax.nn.gelu(jnp.concatenate(ys, axis=1)).astype(_BF)

    acc = cb_ref[...] + cw_ref[pl.ds(CONV_W - 1, 1), :] * c
    for k in range(CONV_W - 1):
        tap = buf_ref[:, k, :]
        acc = acc + cw_ref[pl.ds(k, 1), :] * tap
        if k:
            nbuf_ref[:, k - 1, :] = tap
    nbuf_ref[:, CONV_W - 2, :] = c
    cact_ref[...] = _layernorm_silu(acc, lng_ref[...], lnb_ref[...]).astype(_BF)


def _seq_sample(x, h_re, h_im, buf, new_buf, l, w, w_uc, s5):
    nb = x.shape[0]
    hist = (nb, CONV_W - 1, D_CONV)
    one = pl.Buffered(1)
    full = lambda shape: pl.BlockSpec(shape, lambda i: (0,) * len(shape))
    lc = functools.partial(_layer_const, l, pipeline_mode=one)
    in_specs = [
        full((nb, D_MODEL)), lc((nb, N_STATE)), lc((nb, N_STATE)), lc(hist),
        lc((1, D_MODEL)), lc((D_MODEL, D_SSM + 2 * D_CONV)),
        lc((N_GROUPS, MXU, MXU)), lc((N_GROUPS, MXU, STATE_W)), lc((N_GROUPS, STATE_W, MXU)),
        lc((2, N_GROUPS * STATE_W)), lc((1, D_SSM)),
        lc((CONV_W, D_CONV)), lc((1, D_CONV)), lc((1, D_CONV)), lc((1, D_CONV)),
        pl.BlockSpec(memory_space=pl.ANY),
    ]
    return pl.pallas_call(
        _seq_sample_kernel,
        grid=(1,),
        in_specs=in_specs,
        out_specs=[full((nb, D_SSM)), full((nb, D_CONV)), full((nb, N_STATE)), full((nb, N_STATE)),
                   _layer_const(l, hist)],
        out_shape=[
            jax.ShapeDtypeStruct((nb, D_SSM), _BF),
            jax.ShapeDtypeStruct((nb, D_CONV), _BF),
            jax.ShapeDtypeStruct((nb, N_STATE), _F32),
            jax.ShapeDtypeStruct((nb, N_STATE), _F32),
            jax.ShapeDtypeStruct((DEPTH,) + hist, _F32),
        ],
        input_output_aliases={len(in_specs) - 1: 4},
        compiler_params=pltpu.CompilerParams(
            dimension_semantics=("arbitrary",), vmem_limit_bytes=VMEM_LIMIT),
        name=f"seq_sample_l{l}",
    )(x, h_re, h_im, buf, w['g_mix'], w_uc, s5['toe'], s5['w_sin'], s5['w_sout'], s5['a_1'], w['ssm_d'],
      w['conv_w'], w['conv_b'], w['ln_g'], w['ln_b'], new_buf)


def _merge_and_ffn(x, xn, cact, gy, wgate_ref, wglu_ref, wpw_ref, wout_ref, gffn_ref,
                   wffi_ref, wffo_ref, gfin_ref, final, side_work=(), a_scr=None):
    yg = _dot(gy, wglu_ref[...])
    ya = yg[:, :D_MODEL] * _sigmoid(yg[:, D_MODEL:])
    yb = _dot(cact, wpw_ref[...])
    ga = _sigmoid(_dot(xn, wgate_ref[:, pl.ds(0, D_MODEL)]))
    gb = _sigmoid(_dot(xn, wgate_ref[:, pl.ds(D_MODEL, D_MODEL)]))
    merged = (ga * ya + gb * yb).astype(_BF)
    x1 = x + _dot(merged, wout_ref[...])
    hn = _rms(x1, gffn_ref[...]).astype(_BF)
    acc = x1
    n_chunks = D_FF // FF_CHUNK
    per_chunk = -(-len(side_work) // n_chunks)
    for ck in range(n_chunks):
        h1 = _dot(hn, wffi_ref[:, pl.ds(FF_CHUNK * ck, FF_CHUNK)])
        h2 = _dot(hn, wffi_ref[:, pl.ds(D_FF + FF_CHUNK * ck, FF_CHUNK)])
        a = (h1 * _sigmoid(h1) * h2).astype(_BF)
        if a_scr is None:
            acc = acc + _dot(a, wffo_ref[pl.ds(FF_CHUNK * ck, FF_CHUNK), :])
        else:
            a_scr[:, pl.ds(FF_CHUNK * ck, FF_CHUNK)] = a
        for thunk in side_work[per_chunk * ck:per_chunk * (ck + 1)]:
            thunk()
    if a_scr is not None:
        acc = acc + _dot(a_scr[...], wffo_ref[...])
    if final:
        acc = _rms(acc, gfin_ref[...])
    return acc


def _conv_head(x, gmix_ref, wcv_ref, wcg_ref, cfull_scr, tm, hist_rows):
    xn = _rms(x, gmix_ref[...]).astype(_BF)
    cfull_scr[pl.ds(hist_rows, tm), :] = _dot(xn, wcv_ref[...]) * _sigmoid(_dot(xn, wcg_ref[...]))
    return xn


def _conv_tiles(cw_ref, cb_ref, cfull_scr, conv_scr, tm, nb):
    def tile(t0, lb):
        ls = pl.ds(LANES * lb, LANES)
        acc = jnp.broadcast_to(cb_ref[:, ls], (CONV_TS, LANES))
        for k in range(CONV_W):
            acc = acc + cw_ref[pl.ds(k, 1), ls] * cfull_scr[pl.ds(t0 + nb * (k + HIST_PAD), CONV_TS), ls]
        conv_scr[pl.ds(t0, CONV_TS), ls] = acc
    return [functools.partial(tile, ts * CONV_TS, lb)
            for ts in range(tm // CONV_TS) for lb in range(D_CONV // LANES)]


def _conv_tail(lng_ref, lnb_ref, cfull_scr, conv_scr, tm, hist_rows):
    cact = _layernorm_silu(conv_scr[...], lng_ref[...], lnb_ref[...]).astype(_BF)
    tail = cfull_scr[pl.ds(tm, hist_rows), :]
    cfull_scr[pl.ds(0, hist_rows), :] = tail
    return cact, tail


def _tok_prompt_kernel(x_ref, xnext_ref, gy_ref, gmix_ref, wcv_ref, wcg_ref, cw_ref, cb_ref, lng_ref,
                       lnb_ref, wgate_ref, wglu_ref, wpw_ref, wout_ref, gffn_ref, wffi_ref, wffo_ref,
                       gfin_ref, o_ref, hist_ref, cfull_scr, conv_scr, xn_scr, cact_scr, a_scr,
                       *, tm, nb, final):
    n = pl.program_id(0)
    hist_rows = HIST * nb
    head = functools.partial(_conv_head, gmix_ref=gmix_ref, wcv_ref=wcv_ref, wcg_ref=wcg_ref,
                             cfull_scr=cfull_scr, tm=tm, hist_rows=hist_rows)
    tiles = _conv_tiles(cw_ref, cb_ref, cfull_scr, conv_scr, tm, nb)
    tail = functools.partial(_conv_tail, lng_ref, lnb_ref, cfull_scr, conv_scr, tm, hist_rows)

    @pl.when(n == 0)
    def _():
        cfull_scr[pl.ds(0, hist_rows), :] = jnp.zeros((hist_rows, D_CONV), _F32)
        xn_scr[0] = head(x_ref[...])
        for thunk in tiles:
            thunk()
        cact_scr[0] = tail()[0]

    cur = n % 2
    xn_cur = xn_scr[cur]
    cact_cur = cact_scr[cur]
    xn_scr[1 - cur] = head(xnext_ref[...])
    o_ref[...] = _merge_and_ffn(x_ref[...], xn_cur, cact_cur, gy_ref[...], wgate_ref, wglu_ref,
                                wpw_ref, wout_ref, gffn_ref, wffi_ref, wffo_ref, gfin_ref, final,
                                side_work=tiles, a_scr=a_scr)
    cact_next, last_rows = tail()
    cact_scr[1 - cur] = cact_next
    hist_ref[...] = last_rows


def _tok_weight_specs(l):
    lc = functools.partial(_layer_const, l, pipeline_mode=pl.Buffered(1))
    return dict(
        gmix=lc((1, D_MODEL)), wgate=lc((D_MODEL, 2 * D_MODEL)),
        wglu=lc((D_SSM, 2 * D_MODEL)), wpw=lc((D_CONV, D_MODEL)),
        wout=lc((D_MODEL, D_MODEL)), gffn=lc((1, D_MODEL)), wffi=lc((D_MODEL, 2 * D_FF)),
        wffo=lc((D_FF, D_MODEL)),
        gfin=pl.BlockSpec((1, D_MODEL), lambda *_: (0, 0), pipeline_mode=pl.Buffered(1)))


def _tok_prompt(x, gy, nb, l, w, w_uc, final):
    total = x.shape[0]
    tm = TOK_TM
    hist_rows = HIST * nb
    assert tm >= hist_rows and tm % CONV_TS == 0 and total % tm == 0
    ws = _tok_weight_specs(l)
    lc = functools.partial(_layer_const, l, pipeline_mode=pl.Buffered(1))
    n_tiles = total // tm
    tile = lambda width: pl.BlockSpec((tm, width), lambda n: (n, 0))
    tile_next = pl.BlockSpec((tm, D_MODEL), lambda n: (jnp.minimum(n + 1, n_tiles - 1), 0))
    return pl.pallas_call(
        functools.partial(_tok_prompt_kernel, tm=tm, nb=nb, final=final),
        grid=(n_tiles,),
        in_specs=[tile(D_MODEL), tile_next, tile(D_SSM), ws['gmix'],
                  lc((D_MODEL, D_CONV), index=(0, D_SSM // D_CONV)),
                  lc((D_MODEL, D_CONV), index=(0, D_SSM // D_CONV + 1)),
                  lc((CONV_W, D_CONV)), lc((1, D_CONV)), lc((1, D_CONV)), lc((1, D_CONV)),
                  ws['wgate'], ws['wglu'], ws['wpw'], ws['wout'], ws['gffn'], ws['wffi'], ws['wffo'],
                  ws['gfin']],
        out_specs=[tile(D_MODEL), pl.BlockSpec((hist_rows, D_CONV), lambda n: (0, 0))],
        out_shape=[jax.ShapeDtypeStruct((total, D_MODEL), _F32),
                   jax.ShapeDtypeStruct((hist_rows, D_CONV), _F32)],
        scratch_shapes=[pltpu.VMEM((hist_rows + tm, D_CONV), _F32),
                        pltpu.VMEM((tm, D_CONV), _F32),
                        pltpu.VMEM((2, tm, D_MODEL), _BF),
                        pltpu.VMEM((2, tm, D_CONV), _BF),
                        pltpu.VMEM((tm, D_FF), _BF)],
        compiler_params=pltpu.CompilerParams(
            dimension_semantics=("arbitrary",), vmem_limit_bytes=VMEM_LIMIT),
        name=f"tok_prompt_l{l}",
    )(x, x, gy, w['g_mix'], w_uc, w_uc, w['conv_w'], w['conv_b'], w['ln_g'], w['ln_b'],
      w['w_gate'], w['w_glu'], w['w_pw'], w['w_out'], w['g_ffn'], w['w_ff_in'], w['w_ff_out'], w['g_final'])


def _tok_sample_kernel(x_ref, gy_ref, cact_ref, gmix_ref, wgate_ref, wglu_ref, wpw_ref, wout_ref,
                       gffn_ref, wffi_ref, wffo_ref, gfin_ref, o_ref, *, final):
    x = x_ref[...]
    xn = _rms(x, gmix_ref[...]).astype(_BF)
    o_ref[...] = _merge_and_ffn(x, xn, cact_ref[...], gy_ref[...], wgate_ref, wglu_ref, wpw_ref,
                                wout_ref, gffn_ref, wffi_ref, wffo_ref, gfin_ref, final)


def _tok_sample(x, gy, cact, l, w, final):
    nb = x.shape[0]
    ws = _tok_weight_specs(l)
    full = lambda shape: pl.BlockSpec(shape, lambda i: (0,) * len(shape))
    return pl.pallas_call(
        functools.partial(_tok_sample_kernel, final=final),
        grid=(1,),
        in_specs=[full((nb, D_MODEL)), full((nb, D_SSM)), full((nb, D_CONV)),
                  ws['gmix'], ws['wgate'], ws['wglu'], ws['wpw'], ws['wout'], ws['gffn'], ws['wffi'],
                  ws['wffo'], ws['gfin']],
        out_specs=full((nb, D_MODEL)),
        out_shape=jax.ShapeDtypeStruct((nb, D_MODEL), _F32),
        compiler_params=pltpu.CompilerParams(
            dimension_semantics=("arbitrary",), vmem_limit_bytes=VMEM_LIMIT),
        name=f"tok_sample_l{l}",
    )(x, gy, cact, w['g_mix'], w['w_gate'], w['w_glu'], w['w_pw'], w['w_out'], w['g_ffn'],
      w['w_ff_in'], w['w_ff_out'], w['g_final'])


def kernel(x_prompt, x_sample, state_ssm_re, state_ssm_im, state_conv, g_mix, w_in, ssm_a_re, ssm_a_im,
           ssm_log_dt, ssm_b_re, ssm_b_im, ssm_c_re, ssm_c_im, ssm_d, w_glu, conv_w, conv_b, conv_ln_g,
           conv_ln_b, w_pw, w_out, g_ffn, w_ff_in, w_ff_out, g_final):
    nbp = x_prompt.shape[0]
    nbs = x_sample.shape[0]
    row = lambda a: a.astype(_F32).reshape(DEPTH, 1, -1)
    s5 = _s5_params(ssm_a_re, ssm_a_im, ssm_log_dt, ssm_b_re, ssm_b_im, ssm_c_re, ssm_c_im)
    n_uc = D_SSM + 2 * D_CONV
    w_uc = w_in[:, :, :n_uc].astype(_BF)
    w = dict(
        g_mix=row(g_mix), w_gate=w_in[:, :, n_uc:].astype(_BF), w_glu=w_glu.astype(_BF),
        conv_w=conv_w.astype(_F32), conv_b=row(conv_b), ln_g=row(conv_ln_g), ln_b=row(conv_ln_b),
        w_pw=w_pw.astype(_BF), w_out=w_out.astype(_BF), g_ffn=row(g_ffn),
        w_ff_in=w_ff_in.astype(_BF), w_ff_out=w_ff_out.astype(_BF),
        g_final=g_final.astype(_F32).reshape(1, D_MODEL), ssm_d=row(ssm_d))

    seq = x_prompt.shape[1]
    xp = x_prompt.transpose(1, 0, 2).reshape(seq * nbp, D_MODEL)
    xs = x_sample.reshape(nbs, D_MODEL)
    h_re = state_ssm_re.reshape(DEPTH, nbs, N_STATE)
    h_im = state_ssm_im.reshape(DEPTH, nbs, N_STATE)
    buf = state_conv

    re_p, im_p, conv_p, re_s, im_s = [], [], [], [], []
    conv_s = jnp.zeros(state_conv.shape, _F32)
    for l in range(DEPTH):
        final = l == DEPTH - 1
        gy, hre, him = _seq_prompt(xp, nbp, l, w['g_mix'], w_uc, s5, w['ssm_d'])
        xp, hist = _tok_prompt(xp, gy, nbp, l, w, w_uc, final)
        re_p.append(hre.reshape(nbp, N_GROUPS, P_STATE))
        im_p.append(him.reshape(nbp, N_GROUPS, P_STATE))
        conv_p.append(hist.reshape(HIST, nbp, D_CONV)[HIST_PAD:].transpose(1, 0, 2))

        gys, cacts, nre, nim, conv_s = _seq_sample(xs, h_re, h_im, buf, conv_s, l, w, w_uc, s5)
        xs = _tok_sample(xs, gys, cacts, l, w, final)
        re_s.append(nre.reshape(nbs, N_GROUPS, P_STATE))
        im_s.append(nim.reshape(nbs, N_GROUPS, P_STATE))

    y_prompt = xp.reshape(seq, nbp, D_MODEL).transpose(1, 0, 2)
    return (y_prompt, xs.reshape(nbs, 1, D_MODEL), jnp.stack(re_p), jnp.stack(im_p), jnp.stack(conv_p),
            jnp.stack(re_s), jnp.stack(im_s), conv_s)
```

```python
import functools

import numpy as np
import jax
import jax.numpy as jnp
from jax import lax
from jax.experimental import pallas as pl
from jax.experimental.pallas import tpu as pltpu

D_MODEL = 1024
DEPTH = 4
D_SSM = 512
GROUP = 16
N_GROUPS = 32
P_STATE = 64
D_CONV = 512
CONV_W = 31
D_FF = 2816
EPS = 1e-6

LANES = 128
SUBLANES = 8
MXU = 256
LC = 16
SLOTS = LANES // GROUP
STATE_W = 2 * P_STATE
N_STATE = N_GROUPS * P_STATE
HIST = 32
HIST_PAD = HIST - (CONV_W - 1)
FF_CHUNK = MXU
GATE_W = 512
GATE_COL0 = (D_SSM + 2 * D_CONV) // GATE_W
GATE_BLOCKS = 2 * D_MODEL // GATE_W
VMEM_LIMIT = 58 * 1024 * 1024

SEQ_TT = 256
SEQ_RB = 32
TOK_TM = 512
CONV_TS = 64

_BF = jnp.bfloat16
_F32 = jnp.float32


def _dot(a, b):
    return jnp.dot(a, b, preferred_element_type=_F32)


def _rms(xf, g):
    return xf * lax.rsqrt(jnp.mean(xf * xf, axis=-1, keepdims=True) + EPS) * g


def _sigmoid(x):
    return jax.nn.sigmoid(x)


def _layernorm_silu(c, g, b):
    mu = jnp.mean(c, axis=-1, keepdims=True)
    var = jnp.mean(jnp.square(c - mu), axis=-1, keepdims=True)
    y = (c - mu) * lax.rsqrt(var + EPS) * g + b
    return y * _sigmoid(y)


N_POW = LC + 1
ROW_PW1 = 0
ROW_PW2 = N_POW
ROW_Q1 = 2 * N_POW
ROW_Q2 = 2 * N_POW + 1
TAB_ROWS = 40


def _swap(v):
    return pltpu.roll(v, P_STATE, 1)


def _s5_ops_kernel(tab_ref, bt_ref, cx_ref, toe_ref, wsin_ref, wsout_ref):
    lane = lax.broadcasted_iota(jnp.int32, (GROUP, LANES), 1)
    zeros = jnp.zeros((GROUP, LANES), _F32)

    def one_group(g, g8):
        row = lambda r: tab_ref[g, pl.ds(r, 1), :]
        bt0 = bt_ref[g]
        bbar = bt0 * row(ROW_Q1) + _swap(bt0) * row(ROW_Q2)
        bbar_s = _swap(bbar)
        cx = cx_ref[g]
        cxs = _swap(cx)
        ca = [cx * row(ROW_PW1 + n) - cxs * row(ROW_PW2 + n) for n in range(N_POW)]
        kt = lax.dot_general(bbar, jnp.concatenate(ca[:LC], axis=0), (((1,), (1,)), ((), ())),
                             precision=lax.Precision.HIGHEST, preferred_element_type=_F32)
        kt0, kt1 = kt[:, :LANES], kt[:, LANES:]
        for j in range(LC):
            jh, jl = divmod(j, SLOTS)
            off = LANES * jh + GROUP * ((jl + g8) % SLOTS)
            r0 = pltpu.roll(kt0, GROUP * jl, 1) if jl else kt0
            r1 = pltpu.roll(kt1, GROUP * jl, 1) if jl else kt1
            lo = lane < GROUP * jl
            if jh == 0:
                o0 = jnp.where(lo, 0.0, r0) if jl else r0
                o1 = jnp.where(lo, r0, r1) if jl else r1
            else:
                o0 = zeros
                o1 = jnp.where(lo, 0.0, r0) if jl else r0
            if g8:
                o0 = pltpu.roll(o0, GROUP * g8, 1)
                o1 = pltpu.roll(o1, GROUP * g8, 1)
            toe_ref[g, pl.ds(off, GROUP), :] = jnp.concatenate([o0, o1], axis=1).astype(_BF)
            n = LC - 1 - j
            wsin_ref[g, pl.ds(off, GROUP), :] = (
                bbar * row(ROW_PW1 + n) + bbar_s * row(ROW_PW2 + n)).astype(_BF)
        wt = jnp.concatenate(ca[1:], axis=0).T
        w0, w1 = wt[:, :LANES], wt[:, LANES:]
        if g8:
            w0 = pltpu.roll(w0, GROUP * g8, 1)
            w1 = pltpu.roll(w1, GROUP * g8, 1)
        wsout_ref[g] = jnp.concatenate([w0, w1], axis=1).astype(_BF)

    def octet(q, carry):
        for g8 in range(SLOTS):
            one_group(q * SLOTS + g8, g8)
        return carry

    lax.fori_loop(0, N_GROUPS // SLOTS, octet, 0)


def _s5_params(a_re, a_im, log_dt, b_re, b_im, c_re, c_im):
    a_re, a_im = a_re.astype(_F32), a_im.astype(_F32)
    dt = jnp.exp(log_dt.astype(_F32))[..., None]
    lre, lim = dt * a_re, dt * a_im
    n = jnp.arange(LC + 1, dtype=_F32)[None, None, :, None]
    pmag = jnp.exp(n * lre[:, :, None, :])
    pw_re = pmag * jnp.cos(n * lim[:, :, None, :])
    pw_im = pmag * jnp.sin(n * lim[:, :, None, :])
    ab_re, ab_im = pw_re[:, :, 1], pw_im[:, :, 1]
    nr, ni = ab_re - 1.0, ab_im
    den = a_re * a_re + a_im * a_im
    q_re = (nr * a_re + ni * a_im) / den
    q_im = (ni * a_re - nr * a_im) / den
    both = lambda a, b: jnp.concatenate([a, b], axis=-1)
    pw1 = both(pw_re, pw_re)
    pw2 = both(-pw_im, pw_im)
    tab = jnp.concatenate(
        [pw1, pw2, both(q_re, q_re)[:, :, None], both(-q_im, q_im)[:, :, None],
         jnp.zeros((DEPTH, N_GROUPS, TAB_ROWS - 2 * N_POW - 2, STATE_W), _F32)], axis=2)
    bt0 = both(b_re.astype(_F32).transpose(0, 1, 3, 2), b_im.astype(_F32).transpose(0, 1, 3, 2))
    cx = both(c_re.astype(_F32), -c_im.astype(_F32))

    per_layer = lambda *shape: pl.BlockSpec((None,) + shape, lambda l: (l,) + (0,) * len(shape))
    toe, w_sin, w_sout = pl.pallas_call(
        _s5_ops_kernel,
        grid=(DEPTH,),
        in_specs=[per_layer(N_GROUPS, TAB_ROWS, STATE_W), per_layer(N_GROUPS, GROUP, STATE_W),
                  per_layer(N_GROUPS, GROUP, STATE_W)],
        out_specs=[per_layer(N_GROUPS, MXU, MXU), per_layer(N_GROUPS, MXU, STATE_W),
                   per_layer(N_GROUPS, STATE_W, MXU)],
        out_shape=[jax.ShapeDtypeStruct((DEPTH, N_GROUPS, MXU, MXU), _BF),
                   jax.ShapeDtypeStruct((DEPTH, N_GROUPS, MXU, STATE_W), _BF),
                   jax.ShapeDtypeStruct((DEPTH, N_GROUPS, STATE_W, MXU), _BF)],
        compiler_params=pltpu.CompilerParams(
            dimension_semantics=("arbitrary",), vmem_limit_bytes=VMEM_LIMIT),
        name="s5_operators",
    )(tab, bt0, cx)

    flat = lambda t: t.reshape(DEPTH, 1, N_GROUPS * STATE_W)
    mult_rows = lambda n: jnp.concatenate([flat(pw1[:, :, n]), flat(pw2[:, :, n])], axis=1)
    a_lc = jnp.stack([pw_re[:, :, LC].reshape(DEPTH, N_STATE), pw_im[:, :, LC].reshape(DEPTH, N_STATE)],
                     axis=1)
    return dict(toe=toe, w_sin=w_sin, w_sout=w_sout, a_lc=a_lc, a_1=mult_rows(1))


def _seq_prompt_kernel(x_ref, gmix_ref, wu_ref, toe_ref, wsin_ref, wsout_ref, alc_ref, d_ref,
                       gy_ref, hre_ref, him_ref,
                       u_scr, lhs_scr, e_scr, hs_scr, yg_scr, y_scr, hc_scr, *, tt):
    rows = x_ref.shape[0]
    nb = rows // tt
    m = rows // LC
    nk = tt // LC
    step_rows = lambda k, j: pl.ds((LC * k + j) * nb, nb)

    @pl.when(pl.program_id(0) == 0)
    def _():
        hc_scr[...] = jnp.zeros_like(hc_scr)

    xn = _rms(x_ref[...], gmix_ref[...]).astype(_BF)
    u_scr[...] = _dot(xn, wu_ref[...])

    slot = lax.broadcasted_iota(jnp.int32, (SEQ_RB, LANES), 1) // GROUP
    kb = SEQ_RB // nb

    for v in range(D_SSM // LANES):
        ls = pl.ds(LANES * v, LANES)
        for mb in range(m // SEQ_RB):
            mrows = pl.ds(SEQ_RB * mb, SEQ_RB)
            for hf in range(LC // SLOTS):
                rolled = []
                for jl in range(SLOTS):
                    piece = jnp.concatenate(
                        [u_scr[step_rows(kb * mb + k, SLOTS * hf + jl), ls] for k in range(kb)], axis=0)
                    rolled.append(pltpu.roll(piece, GROUP * jl, 1) if jl else piece)
                for gl in range(SLOTS):
                    acc = rolled[0]
                    for jl in range(1, SLOTS):
                        acc = jnp.where(slot == (gl + jl) % SLOTS, rolled[jl], acc)
                    lhs_scr[SLOTS * v + gl, mrows, pl.ds(LANES * hf, LANES)] = acc.astype(_BF)

    for g in range(N_GROUPS):
        e_scr[g] = _dot(lhs_scr[g], wsin_ref[g])

    lo = lax.broadcasted_iota(jnp.int32, (nb, LANES), 1) < P_STATE
    re_prev, im_prev = hc_scr[0], hc_scr[1]
    re_last, im_last = [], []
    for v in range(N_GROUPS // 2):
        sl = pl.ds(LANES * v, LANES)
        ar, ai = alc_ref[0:1, sl], alc_ref[1:2, sl]
        hr = re_prev[:, LANES * v:LANES * (v + 1)]
        hi = im_prev[:, LANES * v:LANES * (v + 1)]
        for k in range(nk):
            rws = pl.ds(nb * k, nb)
            e_even, e_odd = e_scr[2 * v, rws, :], e_scr[2 * v + 1, rws, :]
            e_re = jnp.where(lo, e_even, _swap(e_odd))
            e_im = jnp.where(lo, _swap(e_even), e_odd)
            hs_scr[2 * v, rws, :] = jnp.where(lo, hr, _swap(hi))
            hs_scr[2 * v + 1, rws, :] = jnp.where(lo, _swap(hr), hi)
            hr, hi = ar * hr - ai * hi + e_re, ar * hi + ai * hr + e_im
        re_last.append(hr)
        im_last.append(hi)
    re_last = jnp.concatenate(re_last, axis=1)
    im_last = jnp.concatenate(im_last, axis=1)
    hc_scr[0] = re_last
    hc_scr[1] = im_last
    hre_ref[...] = re_last
    him_ref[...] = im_last

    for g in range(N_GROUPS):
        yg_scr[g] = _dot(lhs_scr[g], toe_ref[g]) + _dot(hs_scr[g].astype(_BF), wsout_ref[g])

    for v in range(D_SSM // LANES):
        ls = pl.ds(LANES * v, LANES)
        dv = d_ref[:, ls]
        for mb in range(m // SEQ_RB):
            mrows = pl.ds(SEQ_RB * mb, SEQ_RB)
            for hf in range(LC // SLOTS):
                src = [yg_scr[SLOTS * v + gl, mrows, pl.ds(LANES * hf, LANES)] for gl in range(SLOTS)]
                for jl in range(SLOTS):
                    acc = src[0]
                    for gl in range(1, SLOTS):
                        acc = jnp.where(slot == (gl + jl) % SLOTS, src[gl], acc)
                    if jl:
                        acc = pltpu.roll(acc, GROUP * (SLOTS - jl), 1)
                    for k in range(kb):
                        rws = step_rows(kb * mb + k, SLOTS * hf + jl)
                        y_scr[rws, ls] = acc[nb * k:nb * (k + 1)] + dv * u_scr[rws, ls]

    gy_ref[...] = jax.nn.gelu(y_scr[...]).astype(_BF)


def _layer_const(l, shape, index=None, **kw):
    idx = (l,) + (tuple(index) if index is not None else (0,) * len(shape))
    return pl.BlockSpec((None,) + tuple(shape), lambda *_: idx, **kw)


def _seq_prompt(x, nb, l, gmix, w_uc, s5, ssm_d):
    assert nb == SUBLANES
    seq = x.shape[0] // nb
    tt = SEQ_TT
    rows = nb * tt
    m = rows // LC
    lc = functools.partial(_layer_const, l, pipeline_mode=pl.Buffered(1))
    return pl.pallas_call(
        functools.partial(_seq_prompt_kernel, tt=tt),
        grid=(seq // tt,),
        in_specs=[
            pl.BlockSpec((rows, D_MODEL), lambda i: (i, 0)),
            lc((1, D_MODEL)),
            lc((D_MODEL, D_SSM)),
            lc((N_GROUPS, MXU, MXU)),
            lc((N_GROUPS, MXU, STATE_W)),
            lc((N_GROUPS, STATE_W, MXU)),
            lc((2, N_STATE)),
            lc((1, D_SSM)),
        ],
        out_specs=[
            pl.BlockSpec((rows, D_SSM), lambda i: (i, 0)),
            pl.BlockSpec((nb, N_STATE), lambda i: (0, 0)),
            pl.BlockSpec((nb, N_STATE), lambda i: (0, 0)),
        ],
        out_shape=[
            jax.ShapeDtypeStruct((seq * nb, D_SSM), _BF),
            jax.ShapeDtypeStruct((nb, N_STATE), _F32),
            jax.ShapeDtypeStruct((nb, N_STATE), _F32),
        ],
        scratch_shapes=[
            pltpu.VMEM((rows, D_SSM), _F32),
            pltpu.VMEM((N_GROUPS, m, MXU), _BF),
            pltpu.VMEM((N_GROUPS, m, STATE_W), _F32),
            pltpu.VMEM((N_GROUPS, m, STATE_W), _F32),
            pltpu.VMEM((N_GROUPS, m, MXU), _F32),
            pltpu.VMEM((rows, D_SSM), _F32),
            pltpu.VMEM((2, nb, N_STATE), _F32),
        ],
        compiler_params=pltpu.CompilerParams(
            dimension_semantics=("arbitrary",), vmem_limit_bytes=VMEM_LIMIT),
        name=f"seq_prompt_l{l}",
    )(x, gmix, w_uc, s5['toe'], s5['w_sin'], s5['w_sout'], s5['a_lc'], ssm_d)


def _seq_sample_kernel(x_ref, hre_ref, him_ref, buf_ref, gmix_ref, wuc_ref, toe_ref, wsin_ref, wsout_ref,
                       a1_ref, d_ref, cw_ref, cb_ref, lng_ref, lnb_ref, nbuf_all_ref,
                       gy_ref, cact_ref, nre_ref, nim_ref, nbuf_ref):
    del nbuf_all_ref
    nb = x_ref.shape[0]
    xn = _rms(x_ref[...], gmix_ref[...]).astype(_BF)
    z = _dot(xn, wuc_ref[...])
    u = z[:, :D_SSM]
    c = z[:, D_SSM:D_SSM + D_CONV] * _sigmoid(z[:, D_SSM + D_CONV:])

    lane = lax.broadcasted_iota(jnp.int32, (nb, LANES), 1)
    slot = lane // GROUP
    lo = lane < P_STATE
    zero = jnp.zeros((nb, LANES), _F32)
    last = LC - 1

    ys = []
    for v in range(D_SSM // LANES):
        uv = u[:, LANES * v:LANES * (v + 1)]
        u_last = pltpu.roll(uv, GROUP * (last % SLOTS), 1)
        yv = None
        for gp in range(SLOTS // 2):
            sv = (SLOTS * v) // 2 + gp
            rv = hre_ref[:, pl.ds(LANES * sv, LANES)]
            iv = him_ref[:, pl.ds(LANES * sv, LANES)]
            h0s = (jnp.where(lo, rv, pltpu.roll(iv, P_STATE, 1)),
                   jnp.where(lo, pltpu.roll(rv, P_STATE, 1), iv))
            hn = []
            for par in range(2):
                gl = 2 * gp + par
                g = SLOTS * v + gl
                h0 = h0s[par]
                lhs_l = jnp.concatenate(
                    [zero, jnp.where(slot == (gl + last) % SLOTS, u_last, 0.0)], axis=1).astype(_BF)
                e = _dot(lhs_l, wsin_ref[g])
                sl = pl.ds(STATE_W * g, STATE_W)
                hn.append(h0 * a1_ref[0:1, sl] + pltpu.roll(h0, P_STATE, 1) * a1_ref[1:2, sl] + e)
                lhs_f = jnp.concatenate([jnp.where(slot == gl, uv, 0.0), zero], axis=1).astype(_BF)
                yg = (_dot(lhs_f, toe_ref[g, :, pl.ds(0, LANES)])
                      + _dot(h0.astype(_BF), wsout_ref[g, :, pl.ds(0, LANES)]))
                yv = yg if yv is None else jnp.where(slot == gl, yg, yv)
            nre_ref[:, pl.ds(LANES * sv, LANES)] = jnp.where(lo, hn[0], pltpu.roll(hn[1], P_STATE, 1))
            nim_ref[:, pl.ds(LANES * sv, LANES)] = jnp.where(lo, pltpu.roll(hn[0], P_STATE, 1), hn[1])
        ys.append(yv + d_ref[:, pl.ds(LANES * v, LANES)] * uv)
    gy_ref[...] = jax.nn.gelu(jnp.concatenate(ys, axis=1)).astype(_BF)

    acc = cb_ref[...] + cw_ref[pl.ds(CONV_W - 1, 1), :] * c
    for k in range(CONV_W - 1):
        tap = buf_ref[k]
        acc = acc + cw_ref[pl.ds(k, 1), :] * tap
        if k:
            nbuf_ref[k - 1] = tap
    nbuf_ref[CONV_W - 2] = c
    cact_ref[...] = _layernorm_silu(acc, lng_ref[...], lnb_ref[...]).astype(_BF)


def _seq_sample(x, h_re, h_im, buf, new_buf, l, w, w_uc, s5):
    nb = x.shape[0]
    hist = (CONV_W - 1, nb, D_CONV)
    one = pl.Buffered(1)
    full = lambda shape: pl.BlockSpec(shape, lambda i: (0,) * len(shape))
    lc = functools.partial(_layer_const, l, pipeline_mode=one)
    in_specs = [
        full((nb, D_MODEL)), lc((nb, N_STATE)), lc((nb, N_STATE)), lc(hist),
        lc((1, D_MODEL)), lc((D_MODEL, D_SSM + 2 * D_CONV)),
        lc((N_GROUPS, MXU, MXU)), lc((N_GROUPS, MXU, STATE_W)), lc((N_GROUPS, STATE_W, MXU)),
        lc((2, N_GROUPS * STATE_W)), lc((1, D_SSM)),
        lc((CONV_W, D_CONV)), lc((1, D_CONV)), lc((1, D_CONV)), lc((1, D_CONV)),
        pl.BlockSpec(memory_space=pl.ANY),
    ]
    return pl.pallas_call(
        _seq_sample_kernel,
        grid=(1,),
        in_specs=in_specs,
        out_specs=[full((nb, D_SSM)), full((nb, D_CONV)), full((nb, N_STATE)), full((nb, N_STATE)),
                   _layer_const(l, hist)],
        out_shape=[
            jax.ShapeDtypeStruct((nb, D_SSM), _BF),
            jax.ShapeDtypeStruct((nb, D_CONV), _BF),
            jax.ShapeDtypeStruct((nb, N_STATE), _F32),
            jax.ShapeDtypeStruct((nb, N_STATE), _F32),
            jax.ShapeDtypeStruct((DEPTH,) + hist, _F32),
        ],
        input_output_aliases={len(in_specs) - 1: 4},
        compiler_params=pltpu.CompilerParams(
            dimension_semantics=("arbitrary",), vmem_limit_bytes=VMEM_LIMIT),
        name=f"seq_sample_l{l}",
    )(x, h_re, h_im, buf, w['g_mix'], w_uc, s5['toe'], s5['w_sin'], s5['w_sout'], s5['a_1'], w['ssm_d'],
      w['conv_w'], w['conv_b'], w['ln_g'], w['ln_b'], new_buf)


def _merge_and_ffn(x, xn, cact, gy, wgate_refs, wglu_ref, wpw_ref, wout_ref, gffn_ref,
                   wffi_ref, wffo_ref, gfin_ref, a_scr, final, side_work=()):
    yg = _dot(gy, wglu_ref[...])
    ya = yg[:, :D_MODEL] * _sigmoid(yg[:, D_MODEL:])
    yb = _dot(cact, wpw_ref[...])
    gates = [_sigmoid(_dot(xn, r[...])) for r in wgate_refs]
    ga = jnp.concatenate(gates[:GATE_BLOCKS // 2], axis=1)
    gb = jnp.concatenate(gates[GATE_BLOCKS // 2:], axis=1)
    merged = (ga * ya + gb * yb).astype(_BF)
    x1 = x + _dot(merged, wout_ref[...])
    hn = _rms(x1, gffn_ref[...]).astype(_BF)
    n_chunks = D_FF // FF_CHUNK
    per_chunk = -(-len(side_work) // n_chunks)
    for ck in range(n_chunks):
        h1 = _dot(hn, wffi_ref[:, pl.ds(FF_CHUNK * ck, FF_CHUNK)])
        h2 = _dot(hn, wffi_ref[:, pl.ds(D_FF + FF_CHUNK * ck, FF_CHUNK)])
        a_scr[:, pl.ds(FF_CHUNK * ck, FF_CHUNK)] = (h1 * _sigmoid(h1) * h2).astype(_BF)
        for thunk in side_work[per_chunk * ck:per_chunk * (ck + 1)]:
            thunk()
    acc = x1 + _dot(a_scr[...], wffo_ref[...])
    if final:
        acc = _rms(acc, gfin_ref[...])
    return acc


def _conv_head(x, gmix_ref, wcv_ref, wcg_ref, cfull_scr, tm, hist_rows):
    xn = _rms(x, gmix_ref[...]).astype(_BF)
    cfull_scr[pl.ds(hist_rows, tm), :] = _dot(xn, wcv_ref[...]) * _sigmoid(_dot(xn, wcg_ref[...]))
    return xn


def _conv_tiles(cw_ref, cb_ref, cfull_scr, conv_scr, tm, nb):
    def tile(t0, lb):
        ls = pl.ds(LANES * lb, LANES)
        acc = jnp.broadcast_to(cb_ref[:, ls], (CONV_TS, LANES))
        for k in range(CONV_W):
            acc = acc + cw_ref[pl.ds(k, 1), ls] * cfull_scr[pl.ds(t0 + nb * (k + HIST_PAD), CONV_TS), ls]
        conv_scr[pl.ds(t0, CONV_TS), ls] = acc
    return [functools.partial(tile, ts * CONV_TS, lb)
            for ts in range(tm // CONV_TS) for lb in range(D_CONV // LANES)]


def _conv_tail(lng_ref, lnb_ref, cfull_scr, conv_scr, tm, hist_rows):
    cact = _layernorm_silu(conv_scr[...], lng_ref[...], lnb_ref[...]).astype(_BF)
    tail = cfull_scr[pl.ds(tm, hist_rows), :]
    cfull_scr[pl.ds(0, hist_rows), :] = tail
    return cact, tail


def _tok_prompt_kernel(x_ref, xnext_ref, gy_ref, gmix_ref, wcv_ref, wcg_ref, cw_ref, cb_ref, lng_ref,
                       lnb_ref, wg0_ref, wg1_ref, wg2_ref, wg3_ref, wglu_ref, wpw_ref, wout_ref, gffn_ref,
                       wffi_ref, wffo_ref, gfin_ref, o_ref, hist_ref, cfull_scr, conv_scr, xn_scr, cact_scr, a_scr,
                       *, tm, nb, final):
    n = pl.program_id(0)
    hist_rows = HIST * nb
    head = functools.partial(_conv_head, gmix_ref=gmix_ref, wcv_ref=wcv_ref, wcg_ref=wcg_ref,
                             cfull_scr=cfull_scr, tm=tm, hist_rows=hist_rows)
    tiles = _conv_tiles(cw_ref, cb_ref, cfull_scr, conv_scr, tm, nb)
    tail = functools.partial(_conv_tail, lng_ref, lnb_ref, cfull_scr, conv_scr, tm, hist_rows)

    @pl.when(n == 0)
    def _():
        cfull_scr[pl.ds(0, hist_rows), :] = jnp.zeros((hist_rows, D_CONV), _F32)
        xn_scr[0] = head(x_ref[...])
        for thunk in tiles:
            thunk()
        cact_scr[0] = tail()[0]

    cur = n % 2
    xn_cur = xn_scr[cur]
    cact_cur = cact_scr[cur]
    xn_scr[1 - cur] = head(xnext_ref[...])
    o_ref[...] = _merge_and_ffn(x_ref[...], xn_cur, cact_cur, gy_ref[...],
                                (wg0_ref, wg1_ref, wg2_ref, wg3_ref), wglu_ref, wpw_ref, wout_ref,
                                gffn_ref, wffi_ref, wffo_ref, gfin_ref, a_scr, final, side_work=tiles)
    cact_next, last_rows = tail()
    cact_scr[1 - cur] = cact_next
    hist_ref[...] = last_rows


def _tok_weight_specs(l):
    lc = functools.partial(_layer_const, l, pipeline_mode=pl.Buffered(1))
    return dict(
        gmix=lc((1, D_MODEL)),
        wgate=[lc((D_MODEL, GATE_W), index=(0, GATE_COL0 + i)) for i in range(GATE_BLOCKS)],
        wglu=lc((D_SSM, 2 * D_MODEL)), wpw=lc((D_CONV, D_MODEL)),
        wout=lc((D_MODEL, D_MODEL)), gffn=lc((1, D_MODEL)), wffi=lc((D_MODEL, 2 * D_FF)),
        wffo=lc((D_FF, D_MODEL)),
        gfin=pl.BlockSpec((1, D_MODEL), lambda *_: (0, 0), pipeline_mode=pl.Buffered(1)))


def _tok_prompt(x, gy, nb, l, w, w_uc, final):
    total = x.shape[0]
    tm = TOK_TM
    hist_rows = HIST * nb
    assert tm >= hist_rows and tm % CONV_TS == 0 and total % tm == 0
    ws = _tok_weight_specs(l)
    lc = functools.partial(_layer_const, l, pipeline_mode=pl.Buffered(1))
    n_tiles = total // tm
    tile = lambda width: pl.BlockSpec((tm, width), lambda n: (n, 0))
    tile_next = pl.BlockSpec((tm, D_MODEL), lambda n: (jnp.minimum(n + 1, n_tiles - 1), 0))
    return pl.pallas_call(
        functools.partial(_tok_prompt_kernel, tm=tm, nb=nb, final=final),
        grid=(n_tiles,),
        in_specs=[tile(D_MODEL), tile_next, tile(D_SSM), ws['gmix'],
                  lc((D_MODEL, D_CONV), index=(0, D_SSM // D_CONV)),
                  lc((D_MODEL, D_CONV), index=(0, D_SSM // D_CONV + 1)),
                  lc((CONV_W, D_CONV)), lc((1, D_CONV)), lc((1, D_CONV)), lc((1, D_CONV)),
                  *ws['wgate'], ws['wglu'], ws['wpw'], ws['wout'], ws['gffn'], ws['wffi'], ws['wffo'],
                  ws['gfin']],
        out_specs=[tile(D_MODEL), pl.BlockSpec((hist_rows, D_CONV), lambda n: (0, 0))],
        out_shape=[jax.ShapeDtypeStruct((total, D_MODEL), _F32),
                   jax.ShapeDtypeStruct((hist_rows, D_CONV), _F32)],
        scratch_shapes=[pltpu.VMEM((hist_rows + tm, D_CONV), _F32),
                        pltpu.VMEM((tm, D_CONV), _F32),
                        pltpu.VMEM((2, tm, D_MODEL), _BF),
                        pltpu.VMEM((2, tm, D_CONV), _BF),
                        pltpu.VMEM((tm, D_FF), _BF)],
        compiler_params=pltpu.CompilerParams(
            dimension_semantics=("arbitrary",), vmem_limit_bytes=VMEM_LIMIT),
        name=f"tok_prompt_l{l}",
    )(x, x, gy, w['g_mix'], w_uc, w_uc, w['conv_w'], w['conv_b'], w['ln_g'], w['ln_b'],
      *[w_uc] * GATE_BLOCKS, w['w_glu'], w['w_pw'], w['w_out'], w['g_ffn'], w['w_ff_in'], w['w_ff_out'],
      w['g_final'])


def _tok_sample_kernel(x_ref, gy_ref, cact_ref, gmix_ref, wg0_ref, wg1_ref, wg2_ref, wg3_ref, wglu_ref,
                       wpw_ref, wout_ref, gffn_ref, wffi_ref, wffo_ref, gfin_ref, o_ref, a_scr, *, final):
    x = x_ref[...]
    xn = _rms(x, gmix_ref[...]).astype(_BF)
    o_ref[...] = _merge_and_ffn(x, xn, cact_ref[...], gy_ref[...], (wg0_ref, wg1_ref, wg2_ref, wg3_ref),
                                wglu_ref, wpw_ref, wout_ref, gffn_ref, wffi_ref, wffo_ref, gfin_ref,
                                a_scr, final)


def _tok_sample(x, gy, cact, l, w, w_uc, final):
    nb = x.shape[0]
    ws = _tok_weight_specs(l)
    full = lambda shape: pl.BlockSpec(shape, lambda i: (0,) * len(shape))
    return pl.pallas_call(
        functools.partial(_tok_sample_kernel, final=final),
        grid=(1,),
        in_specs=[full((nb, D_MODEL)), full((nb, D_SSM)), full((nb, D_CONV)),
                  ws['gmix'], *ws['wgate'], ws['wglu'], ws['wpw'], ws['wout'], ws['gffn'], ws['wffi'],
                  ws['wffo'], ws['gfin']],
        out_specs=full((nb, D_MODEL)),
        out_shape=jax.ShapeDtypeStruct((nb, D_MODEL), _F32),
        scratch_shapes=[pltpu.VMEM((nb, D_FF), _BF)],
        compiler_params=pltpu.CompilerParams(
            dimension_semantics=("arbitrary",), vmem_limit_bytes=VMEM_LIMIT),
        name=f"tok_sample_l{l}",
    )(x, gy, cact, w['g_mix'], *[w_uc] * GATE_BLOCKS, w['w_glu'], w['w_pw'], w['w_out'], w['g_ffn'],
      w['w_ff_in'], w['w_ff_out'], w['g_final'])


def kernel(x_prompt, x_sample, state_ssm_re, state_ssm_im, state_conv, g_mix, w_in, ssm_a_re, ssm_a_im,
           ssm_log_dt, ssm_b_re, ssm_b_im, ssm_c_re, ssm_c_im, ssm_d, w_glu, conv_w, conv_b, conv_ln_g,
           conv_ln_b, w_pw, w_out, g_ffn, w_ff_in, w_ff_out, g_final):
    nbp = x_prompt.shape[0]
    nbs = x_sample.shape[0]
    row = lambda a: a.astype(_F32).reshape(DEPTH, 1, -1)
    s5 = _s5_params(ssm_a_re, ssm_a_im, ssm_log_dt, ssm_b_re, ssm_b_im, ssm_c_re, ssm_c_im)
    w_uc = w_in.astype(_BF)
    w = dict(
        g_mix=row(g_mix), w_glu=w_glu.astype(_BF),
        conv_w=conv_w.astype(_F32), conv_b=row(conv_b), ln_g=row(conv_ln_g), ln_b=row(conv_ln_b),
        w_pw=w_pw.astype(_BF), w_out=w_out.astype(_BF), g_ffn=row(g_ffn),
        w_ff_in=w_ff_in.astype(_BF), w_ff_out=w_ff_out.astype(_BF),
        g_final=g_final.astype(_F32).reshape(1, D_MODEL), ssm_d=row(ssm_d))

    seq = x_prompt.shape[1]
    xp = x_prompt.transpose(1, 0, 2).reshape(seq * nbp, D_MODEL)
    xs = x_sample.reshape(nbs, D_MODEL)
    h_re = state_ssm_re.reshape(DEPTH, nbs, N_STATE)
    h_im = state_ssm_im.reshape(DEPTH, nbs, N_STATE)
    buf = state_conv.transpose(0, 2, 1, 3)

    re_p, im_p, conv_p, re_s, im_s = [], [], [], [], []
    conv_s = jnp.zeros(buf.shape, _F32)
    for l in range(DEPTH):
        final = l == DEPTH - 1
        gy, hre, him = _seq_prompt(xp, nbp, l, w['g_mix'], w_uc, s5, w['ssm_d'])
        xp, hist = _tok_prompt(xp, gy, nbp, l, w, w_uc, final)
        re_p.append(hre.reshape(nbp, N_GROUPS, P_STATE))
        im_p.append(him.reshape(nbp, N_GROUPS, P_STATE))
        conv_p.append(hist.reshape(HIST, nbp, D_CONV)[HIST_PAD:].transpose(1, 0, 2))

        gys, cacts, nre, nim, conv_s = _seq_sample(xs, h_re, h_im, buf, conv_s, l, w, w_uc, s5)
        xs = _tok_sample(xs, gys, cacts, l, w, w_uc, final)
        re_s.append(nre.reshape(nbs, N_GROUPS, P_STATE))
        im_s.append(nim.reshape(nbs, N_GROUPS, P_STATE))

    y_prompt = xp.reshape(seq, nbp, D_MODEL).transpose(1, 0, 2)
    return (y_prompt, xs.reshape(nbs, 1, D_MODEL), jnp.stack(re_p), jnp.stack(im_p), jnp.stack(conv_p),
            jnp.stack(re_s), jnp.stack(im_s), conv_s.transpose(0, 2, 1, 3))
```

```python
import functools

import numpy as np
import jax
import jax.numpy as jnp
from jax import lax
from jax.experimental import pallas as pl
from jax.experimental.pallas import tpu as pltpu

D_MODEL = 1024
DEPTH = 4
D_SSM = 512
GROUP = 16
N_GROUPS = 32
P_STATE = 64
D_CONV = 512
CONV_W = 31
D_FF = 2816
EPS = 1e-6

LANES = 128
SUBLANES = 8
MXU = 256
LC = 16
SLOTS = LANES // GROUP
STATE_W = 2 * P_STATE
N_STATE = N_GROUPS * P_STATE
HIST = 32
HIST_PAD = HIST - (CONV_W - 1)
FF_CHUNK = MXU
GATE_W = 512
GATE_COL0 = (D_SSM + 2 * D_CONV) // GATE_W
GATE_BLOCKS = 2 * D_MODEL // GATE_W
VMEM_LIMIT = 58 * 1024 * 1024

SEQ_TT = 256
SEQ_RB = 32
TOK_TM = 512
CONV_TS = 64

_BF = jnp.bfloat16
_F32 = jnp.float32


def _dot(a, b):
    return jnp.dot(a, b, preferred_element_type=_F32)


def _rms(xf, g):
    return xf * lax.rsqrt(jnp.mean(xf * xf, axis=-1, keepdims=True) + EPS) * g


def _sigmoid(x):
    return jax.nn.sigmoid(x)


def _layernorm_silu(c, g, b):
    mu = jnp.mean(c, axis=-1, keepdims=True)
    var = jnp.mean(jnp.square(c - mu), axis=-1, keepdims=True)
    y = (c - mu) * lax.rsqrt(var + EPS) * g + b
    return y * _sigmoid(y)


N_POW = LC + 1
ROW_PW1 = 0
ROW_PW2 = N_POW
ROW_Q1 = 2 * N_POW
ROW_Q2 = 2 * N_POW + 1
TAB_ROWS = 40


def _swap(v):
    return pltpu.roll(v, P_STATE, 1)


def _s5_ops_kernel(tab_ref, bt_ref, cx_ref, toe_ref, wsin_ref, wsout_ref):
    lane = lax.broadcasted_iota(jnp.int32, (GROUP, LANES), 1)
    zeros = jnp.zeros((GROUP, LANES), _F32)

    def one_group(g, g8):
        row = lambda r: tab_ref[g, pl.ds(r, 1), :]
        bt0 = bt_ref[g]
        bbar = bt0 * row(ROW_Q1) + _swap(bt0) * row(ROW_Q2)
        bbar_s = _swap(bbar)
        cx = cx_ref[g]
        cxs = _swap(cx)
        ca = [cx * row(ROW_PW1 + n) - cxs * row(ROW_PW2 + n) for n in range(N_POW)]
        kt = lax.dot_general(bbar, jnp.concatenate(ca[:LC], axis=0), (((1,), (1,)), ((), ())),
                             precision=lax.Precision.HIGHEST, preferred_element_type=_F32)
        kt0, kt1 = kt[:, :LANES], kt[:, LANES:]
        for j in range(LC):
            jh, jl = divmod(j, SLOTS)
            off = LANES * jh + GROUP * ((jl + g8) % SLOTS)
            r0 = pltpu.roll(kt0, GROUP * jl, 1) if jl else kt0
            r1 = pltpu.roll(kt1, GROUP * jl, 1) if jl else kt1
            lo = lane < GROUP * jl
            if jh == 0:
                o0 = jnp.where(lo, 0.0, r0) if jl else r0
                o1 = jnp.where(lo, r0, r1) if jl else r1
            else:
                o0 = zeros
                o1 = jnp.where(lo, 0.0, r0) if jl else r0
            if g8:
                o0 = pltpu.roll(o0, GROUP * g8, 1)
                o1 = pltpu.roll(o1, GROUP * g8, 1)
            toe_ref[g, pl.ds(off, GROUP), :] = jnp.concatenate([o0, o1], axis=1).astype(_BF)
            n = LC - 1 - j
            wsin_ref[g, pl.ds(off, GROUP), :] = (
                bbar * row(ROW_PW1 + n) + bbar_s * row(ROW_PW2 + n)).astype(_BF)
        wt = jnp.concatenate(ca[1:], axis=0).T
        w0, w1 = wt[:, :LANES], wt[:, LANES:]
        if g8:
            w0 = pltpu.roll(w0, GROUP * g8, 1)
            w1 = pltpu.roll(w1, GROUP * g8, 1)
        wsout_ref[g] = jnp.concatenate([w0, w1], axis=1).astype(_BF)

    def octet(q, carry):
        for g8 in range(SLOTS):
            one_group(q * SLOTS + g8, g8)
        return carry

    lax.fori_loop(0, N_GROUPS // SLOTS, octet, 0)


def _s5_params(a_re, a_im, log_dt, b_re, b_im, c_re, c_im):
    a_re, a_im = a_re.astype(_F32), a_im.astype(_F32)
    dt = jnp.exp(log_dt.astype(_F32))[..., None]
    lre, lim = dt * a_re, dt * a_im
    n = jnp.arange(LC + 1, dtype=_F32)[None, None, :, None]
    pmag = jnp.exp(n * lre[:, :, None, :])
    pw_re = pmag * jnp.cos(n * lim[:, :, None, :])
    pw_im = pmag * jnp.sin(n * lim[:, :, None, :])
    ab_re, ab_im = pw_re[:, :, 1], pw_im[:, :, 1]
    nr, ni = ab_re - 1.0, ab_im
    den = a_re * a_re + a_im * a_im
    q_re = (nr * a_re + ni * a_im) / den
    q_im = (ni * a_re - nr * a_im) / den
    both = lambda a, b: jnp.concatenate([a, b], axis=-1)
    pw1 = both(pw_re, pw_re)
    pw2 = both(-pw_im, pw_im)
    tab = jnp.concatenate(
        [pw1, pw2, both(q_re, q_re)[:, :, None], both(-q_im, q_im)[:, :, None],
         jnp.zeros((DEPTH, N_GROUPS, TAB_ROWS - 2 * N_POW - 2, STATE_W), _F32)], axis=2)
    bt0 = both(b_re.astype(_F32).transpose(0, 1, 3, 2), b_im.astype(_F32).transpose(0, 1, 3, 2))
    cx = both(c_re.astype(_F32), -c_im.astype(_F32))

    per_layer = lambda *shape: pl.BlockSpec((None,) + shape, lambda l: (l,) + (0,) * len(shape))
    toe, w_sin, w_sout = pl.pallas_call(
        _s5_ops_kernel,
        grid=(DEPTH,),
        in_specs=[per_layer(N_GROUPS, TAB_ROWS, STATE_W), per_layer(N_GROUPS, GROUP, STATE_W),
                  per_layer(N_GROUPS, GROUP, STATE_W)],
        out_specs=[per_layer(N_GROUPS, MXU, MXU), per_layer(N_GROUPS, MXU, STATE_W),
                   per_layer(N_GROUPS, STATE_W, MXU)],
        out_shape=[jax.ShapeDtypeStruct((DEPTH, N_GROUPS, MXU, MXU), _BF),
                   jax.ShapeDtypeStruct((DEPTH, N_GROUPS, MXU, STATE_W), _BF),
                   jax.ShapeDtypeStruct((DEPTH, N_GROUPS, STATE_W, MXU), _BF)],
        compiler_params=pltpu.CompilerParams(
            dimension_semantics=("arbitrary",), vmem_limit_bytes=VMEM_LIMIT),
        name="s5_operators",
    )(tab, bt0, cx)

    flat = lambda t: t.reshape(DEPTH, 1, N_GROUPS * STATE_W)
    mult_rows = lambda n: jnp.concatenate([flat(pw1[:, :, n]), flat(pw2[:, :, n])], axis=1)
    a_lc = jnp.stack([pw_re[:, :, LC].reshape(DEPTH, N_STATE), pw_im[:, :, LC].reshape(DEPTH, N_STATE)],
                     axis=1)
    return dict(toe=toe, w_sin=w_sin, w_sout=w_sout, a_lc=a_lc, a_1=mult_rows(1))


def _seq_prompt_kernel(x_ref, gmix_ref, wu_ref, toe_ref, wsin_ref, wsout_ref, alc_ref, d_ref,
                       gy_ref, hre_ref, him_ref,
                       u_scr, lhs_scr, e_scr, hs_scr, yg_scr, y_scr, hc_scr, *, tt):
    rows = x_ref.shape[0]
    nb = rows // tt
    m = rows // LC
    nk = tt // LC
    step_rows = lambda k, j: pl.ds((LC * k + j) * nb, nb)

    @pl.when(pl.program_id(0) == 0)
    def _():
        hc_scr[...] = jnp.zeros_like(hc_scr)

    xn = _rms(x_ref[...], gmix_ref[...]).astype(_BF)
    u_scr[...] = _dot(xn, wu_ref[...])

    slot = lax.broadcasted_iota(jnp.int32, (SEQ_RB, LANES), 1) // GROUP
    kb = SEQ_RB // nb

    for v in range(D_SSM // LANES):
        ls = pl.ds(LANES * v, LANES)
        for mb in range(m // SEQ_RB):
            mrows = pl.ds(SEQ_RB * mb, SEQ_RB)
            for hf in range(LC // SLOTS):
                rolled = []
                for jl in range(SLOTS):
                    piece = jnp.concatenate(
                        [u_scr[step_rows(kb * mb + k, SLOTS * hf + jl), ls] for k in range(kb)],
                        axis=0).astype(_BF)
                    rolled.append(pltpu.roll(piece, GROUP * jl, 1) if jl else piece)
                for gl in range(SLOTS):
                    acc = rolled[0]
                    for jl in range(1, SLOTS):
                        acc = jnp.where(slot == (gl + jl) % SLOTS, rolled[jl], acc)
                    lhs_scr[SLOTS * v + gl, mrows, pl.ds(LANES * hf, LANES)] = acc

    for g in range(N_GROUPS):
        e_scr[g] = _dot(lhs_scr[g], wsin_ref[g])

    lo = lax.broadcasted_iota(jnp.int32, (nb, LANES), 1) < P_STATE
    re_prev, im_prev = hc_scr[0], hc_scr[1]
    re_last, im_last = [], []
    for v in range(N_GROUPS // 2):
        sl = pl.ds(LANES * v, LANES)
        ar, ai = alc_ref[0:1, sl], alc_ref[1:2, sl]
        hr = re_prev[:, LANES * v:LANES * (v + 1)]
        hi = im_prev[:, LANES * v:LANES * (v + 1)]
        for k in range(nk):
            rws = pl.ds(nb * k, nb)
            e_even, e_odd = e_scr[2 * v, rws, :], e_scr[2 * v + 1, rws, :]
            e_re = jnp.where(lo, e_even, _swap(e_odd))
            e_im = jnp.where(lo, _swap(e_even), e_odd)
            hs_scr[2 * v, rws, :] = jnp.where(lo, hr, _swap(hi))
            hs_scr[2 * v + 1, rws, :] = jnp.where(lo, _swap(hr), hi)
            hr, hi = ar * hr - ai * hi + e_re, ar * hi + ai * hr + e_im
        re_last.append(hr)
        im_last.append(hi)
    re_last = jnp.concatenate(re_last, axis=1)
    im_last = jnp.concatenate(im_last, axis=1)
    hc_scr[0] = re_last
    hc_scr[1] = im_last
    hre_ref[...] = re_last
    him_ref[...] = im_last

    for g in range(N_GROUPS):
        yg_scr[g] = _dot(lhs_scr[g], toe_ref[g]) + _dot(hs_scr[g].astype(_BF), wsout_ref[g])

    for v in range(D_SSM // LANES):
        ls = pl.ds(LANES * v, LANES)
        dv = d_ref[:, ls]
        for mb in range(m // SEQ_RB):
            mrows = pl.ds(SEQ_RB * mb, SEQ_RB)
            for hf in range(LC // SLOTS):
                src = [yg_scr[SLOTS * v + gl, mrows, pl.ds(LANES * hf, LANES)] for gl in range(SLOTS)]
                for jl in range(SLOTS):
                    acc = src[0]
                    for gl in range(1, SLOTS):
                        acc = jnp.where(slot == (gl + jl) % SLOTS, src[gl], acc)
                    if jl:
                        acc = pltpu.roll(acc, GROUP * (SLOTS - jl), 1)
                    for k in range(kb):
                        rws = step_rows(kb * mb + k, SLOTS * hf + jl)
                        y_scr[rws, ls] = acc[nb * k:nb * (k + 1)] + dv * u_scr[rws, ls]

    gy_ref[...] = jax.nn.gelu(y_scr[...]).astype(_BF)


def _layer_const(l, shape, index=None, **kw):
    idx = (l,) + (tuple(index) if index is not None else (0,) * len(shape))
    return pl.BlockSpec((None,) + tuple(shape), lambda *_: idx, **kw)


def _seq_prompt(x, nb, l, gmix, w_uc, s5, ssm_d):
    assert nb == SUBLANES
    seq = x.shape[0] // nb
    tt = SEQ_TT
    rows = nb * tt
    m = rows // LC
    lc = functools.partial(_layer_const, l, pipeline_mode=pl.Buffered(1))
    return pl.pallas_call(
        functools.partial(_seq_prompt_kernel, tt=tt),
        grid=(seq // tt,),
        in_specs=[
            pl.BlockSpec((rows, D_MODEL), lambda i: (i, 0)),
            lc((1, D_MODEL)),
            lc((D_MODEL, D_SSM)),
            lc((N_GROUPS, MXU, MXU)),
            lc((N_GROUPS, MXU, STATE_W)),
            lc((N_GROUPS, STATE_W, MXU)),
            lc((2, N_STATE)),
            lc((1, D_SSM)),
        ],
        out_specs=[
            pl.BlockSpec((rows, D_SSM), lambda i: (i, 0)),
            pl.BlockSpec((nb, N_STATE), lambda i: (0, 0)),
            pl.BlockSpec((nb, N_STATE), lambda i: (0, 0)),
        ],
        out_shape=[
            jax.ShapeDtypeStruct((seq * nb, D_SSM), _BF),
            jax.ShapeDtypeStruct((nb, N_STATE), _F32),
            jax.ShapeDtypeStruct((nb, N_STATE), _F32),
        ],
        scratch_shapes=[
            pltpu.VMEM((rows, D_SSM), _F32),
            pltpu.VMEM((N_GROUPS, m, MXU), _BF),
            pltpu.VMEM((N_GROUPS, m, STATE_W), _F32),
            pltpu.VMEM((N_GROUPS, m, STATE_W), _F32),
            pltpu.VMEM((N_GROUPS, m, MXU), _F32),
            pltpu.VMEM((rows, D_SSM), _F32),
            pltpu.VMEM((2, nb, N_STATE), _F32),
        ],
        compiler_params=pltpu.CompilerParams(
            dimension_semantics=("arbitrary",), vmem_limit_bytes=VMEM_LIMIT),
        name=f"seq_prompt_l{l}",
    )(x, gmix, w_uc, s5['toe'], s5['w_sin'], s5['w_sout'], s5['a_lc'], ssm_d)


def _seq_sample_kernel(x_ref, hre_ref, him_ref, buf_ref, gmix_ref, wuc_ref, toe_ref, wsin_ref, wsout_ref,
                       a1_ref, d_ref, cw_ref, cb_ref, lng_ref, lnb_ref, nbuf_all_ref,
                       gy_ref, cact_ref, nre_ref, nim_ref, nbuf_ref):
    del nbuf_all_ref
    nb = x_ref.shape[0]
    xn = _rms(x_ref[...], gmix_ref[...]).astype(_BF)
    z = _dot(xn, wuc_ref[...])
    u = z[:, :D_SSM]
    c = z[:, D_SSM:D_SSM + D_CONV] * _sigmoid(z[:, D_SSM + D_CONV:])

    lane = lax.broadcasted_iota(jnp.int32, (nb, LANES), 1)
    slot = lane // GROUP
    lo = lane < P_STATE
    zero = jnp.zeros((nb, LANES), _F32)
    last = LC - 1

    ys = []
    for v in range(D_SSM // LANES):
        uv = u[:, LANES * v:LANES * (v + 1)]
        u_last = pltpu.roll(uv, GROUP * (last % SLOTS), 1)
        yv = None
        for gp in range(SLOTS // 2):
            sv = (SLOTS * v) // 2 + gp
            rv = hre_ref[:, pl.ds(LANES * sv, LANES)]
            iv = him_ref[:, pl.ds(LANES * sv, LANES)]
            h0s = (jnp.where(lo, rv, pltpu.roll(iv, P_STATE, 1)),
                   jnp.where(lo, pltpu.roll(rv, P_STATE, 1), iv))
            hn = []
            for par in range(2):
                gl = 2 * gp + par
                g = SLOTS * v + gl
                h0 = h0s[par]
                lhs_l = jnp.concatenate(
                    [zero, jnp.where(slot == (gl + last) % SLOTS, u_last, 0.0)], axis=1).astype(_BF)
                e = _dot(lhs_l, wsin_ref[g])
                sl = pl.ds(STATE_W * g, STATE_W)
                hn.append(h0 * a1_ref[0:1, sl] + pltpu.roll(h0, P_STATE, 1) * a1_ref[1:2, sl] + e)
                lhs_f = jnp.concatenate([jnp.where(slot == gl, uv, 0.0), zero], axis=1).astype(_BF)
                yg = (_dot(lhs_f, toe_ref[g, :, pl.ds(0, LANES)])
                      + _dot(h0.astype(_BF), wsout_ref[g, :, pl.ds(0, LANES)]))
                yv = yg if yv is None else jnp.where(slot == gl, yg, yv)
            nre_ref[:, pl.ds(LANES * sv, LANES)] = jnp.where(lo, hn[0], pltpu.roll(hn[1], P_STATE, 1))
            nim_ref[:, pl.ds(LANES * sv, LANES)] = jnp.where(lo, pltpu.roll(hn[0], P_STATE, 1), hn[1])
        ys.append(yv + d_ref[:, pl.ds(LANES * v, LANES)] * uv)
    gy_ref[...] = jax.nn.gelu(jnp.concatenate(ys, axis=1)).astype(_BF)

    acc = cb_ref[...] + cw_ref[pl.ds(CONV_W - 1, 1), :] * c
    for k in range(CONV_W - 1):
        tap = buf_ref[k]
        acc = acc + cw_ref[pl.ds(k, 1), :] * tap
        if k:
            nbuf_ref[k - 1] = tap
    nbuf_ref[CONV_W - 2] = c
    cact_ref[...] = _layernorm_silu(acc, lng_ref[...], lnb_ref[...]).astype(_BF)


def _seq_sample(x, h_re, h_im, buf, new_buf, l, w, w_uc, s5):
    nb = x.shape[0]
    hist = (CONV_W - 1, nb, D_CONV)
    one = pl.Buffered(1)
    full = lambda shape: pl.BlockSpec(shape, lambda i: (0,) * len(shape))
    lc = functools.partial(_layer_const, l, pipeline_mode=one)
    in_specs = [
        full((nb, D_MODEL)), lc((nb, N_STATE)), lc((nb, N_STATE)), lc(hist),
        lc((1, D_MODEL)), lc((D_MODEL, D_SSM + 2 * D_CONV)),
        lc((N_GROUPS, MXU, MXU)), lc((N_GROUPS, MXU, STATE_W)), lc((N_GROUPS, STATE_W, MXU)),
        lc((2, N_GROUPS * STATE_W)), lc((1, D_SSM)),
        lc((CONV_W, D_CONV)), lc((1, D_CONV)), lc((1, D_CONV)), lc((1, D_CONV)),
        pl.BlockSpec(memory_space=pl.ANY),
    ]
    return pl.pallas_call(
        _seq_sample_kernel,
        grid=(1,),
        in_specs=in_specs,
        out_specs=[full((nb, D_SSM)), full((nb, D_CONV)), full((nb, N_STATE)), full((nb, N_STATE)),
                   _layer_const(l, hist)],
        out_shape=[
            jax.ShapeDtypeStruct((nb, D_SSM), _BF),
            jax.ShapeDtypeStruct((nb, D_CONV), _BF),
            jax.ShapeDtypeStruct((nb, N_STATE), _F32),
            jax.ShapeDtypeStruct((nb, N_STATE), _F32),
            jax.ShapeDtypeStruct((DEPTH,) + hist, _F32),
        ],
        input_output_aliases={len(in_specs) - 1: 4},
        compiler_params=pltpu.CompilerParams(
            dimension_semantics=("arbitrary",), vmem_limit_bytes=VMEM_LIMIT),
        name=f"seq_sample_l{l}",
    )(x, h_re, h_im, buf, w['g_mix'], w_uc, s5['toe'], s5['w_sin'], s5['w_sout'], s5['a_1'], w['ssm_d'],
      w['conv_w'], w['conv_b'], w['ln_g'], w['ln_b'], new_buf)


def _merge_and_ffn(x, xn, cact, gy, wgate_refs, wglu_ref, wpw_ref, wout_ref, gffn_ref,
                   wffi_ref, wffo_ref, gfin_ref, a_scr, final, side_work=()):
    yg = _dot(gy, wglu_ref[...])
    ya = yg[:, :D_MODEL] * _sigmoid(yg[:, D_MODEL:])
    yb = _dot(cact, wpw_ref[...])
    gates = [_sigmoid(_dot(xn, r[...])) for r in wgate_refs]
    ga = jnp.concatenate(gates[:GATE_BLOCKS // 2], axis=1)
    gb = jnp.concatenate(gates[GATE_BLOCKS // 2:], axis=1)
    merged = (ga * ya + gb * yb).astype(_BF)
    x1 = x + _dot(merged, wout_ref[...])
    hn = _rms(x1, gffn_ref[...]).astype(_BF)
    n_chunks = D_FF // FF_CHUNK
    per_chunk = -(-len(side_work) // n_chunks)
    for ck in range(n_chunks):
        h1 = _dot(hn, wffi_ref[:, pl.ds(FF_CHUNK * ck, FF_CHUNK)])
        h2 = _dot(hn, wffi_ref[:, pl.ds(D_FF + FF_CHUNK * ck, FF_CHUNK)])
        a_scr[:, pl.ds(FF_CHUNK * ck, FF_CHUNK)] = (h1 * _sigmoid(h1) * h2).astype(_BF)
        for thunk in side_work[per_chunk * ck:per_chunk * (ck + 1)]:
            thunk()
    acc = x1 + _dot(a_scr[...], wffo_ref[...])
    if final:
        acc = _rms(acc, gfin_ref[...])
    return acc


def _conv_head(x, gmix_ref, wcv_ref, wcg_ref, cfull_scr, tm, hist_rows):
    xn = _rms(x, gmix_ref[...]).astype(_BF)
    cfull_scr[pl.ds(hist_rows, tm), :] = _dot(xn, wcv_ref[...]) * _sigmoid(_dot(xn, wcg_ref[...]))
    return xn


def _conv_tiles(cw_ref, cb_ref, cfull_scr, conv_scr, tm, nb):
    def tile(t0, lb):
        ls = pl.ds(LANES * lb, LANES)
        acc = jnp.broadcast_to(cb_ref[:, ls], (CONV_TS, LANES))
        for k in range(CONV_W):
            acc = acc + cw_ref[pl.ds(k, 1), ls] * cfull_scr[pl.ds(t0 + nb * (k + HIST_PAD), CONV_TS), ls]
        conv_scr[pl.ds(t0, CONV_TS), ls] = acc
    return [functools.partial(tile, ts * CONV_TS, lb)
            for ts in range(tm // CONV_TS) for lb in range(D_CONV // LANES)]


def _conv_tail(lng_ref, lnb_ref, cfull_scr, conv_scr, tm, hist_rows):
    cact = _layernorm_silu(conv_scr[...], lng_ref[...], lnb_ref[...]).astype(_BF)
    tail = cfull_scr[pl.ds(tm, hist_rows), :]
    cfull_scr[pl.ds(0, hist_rows), :] = tail
    return cact, tail


def _tok_prompt_kernel(x_ref, xnext_ref, gy_ref, gmix_ref, wcv_ref, wcg_ref, cw_ref, cb_ref, lng_ref,
                       lnb_ref, wg0_ref, wg1_ref, wg2_ref, wg3_ref, wglu_ref, wpw_ref, wout_ref, gffn_ref,
                       wffi_ref, wffo_ref, gfin_ref, o_ref, hist_ref, cfull_scr, conv_scr, xn_scr, cact_scr, a_scr,
                       *, tm, nb, final):
    n = pl.program_id(0)
    hist_rows = HIST * nb
    head = functools.partial(_conv_head, gmix_ref=gmix_ref, wcv_ref=wcv_ref, wcg_ref=wcg_ref,
                             cfull_scr=cfull_scr, tm=tm, hist_rows=hist_rows)
    tiles = _conv_tiles(cw_ref, cb_ref, cfull_scr, conv_scr, tm, nb)
    tail = functools.partial(_conv_tail, lng_ref, lnb_ref, cfull_scr, conv_scr, tm, hist_rows)

    @pl.when(n == 0)
    def _():
        cfull_scr[pl.ds(0, hist_rows), :] = jnp.zeros((hist_rows, D_CONV), _F32)
        xn_scr[0] = head(x_ref[...])
        for thunk in tiles:
            thunk()
        cact_scr[0] = tail()[0]

    cur = n % 2
    xn_cur = xn_scr[cur]
    cact_cur = cact_scr[cur]
    xn_scr[1 - cur] = head(xnext_ref[...])
    o_ref[...] = _merge_and_ffn(x_ref[...], xn_cur, cact_cur, gy_ref[...],
                                (wg0_ref, wg1_ref, wg2_ref, wg3_ref), wglu_ref, wpw_ref, wout_ref,
                                gffn_ref, wffi_ref, wffo_ref, gfin_ref, a_scr, final, side_work=tiles)
    cact_next, last_rows = tail()
    cact_scr[1 - cur] = cact_next
    hist_ref[...] = last_rows


def _tok_weight_specs(l):
    lc = functools.partial(_layer_const, l, pipeline_mode=pl.Buffered(1))
    return dict(
        gmix=lc((1, D_MODEL)),
        wgate=[lc((D_MODEL, GATE_W), index=(0, GATE_COL0 + i)) for i in range(GATE_BLOCKS)],
        wglu=lc((D_SSM, 2 * D_MODEL)), wpw=lc((D_CONV, D_MODEL)),
        wout=lc((D_MODEL, D_MODEL)), gffn=lc((1, D_MODEL)), wffi=lc((D_MODEL, 2 * D_FF)),
        wffo=lc((D_FF, D_MODEL)),
        gfin=pl.BlockSpec((1, D_MODEL), lambda *_: (0, 0), pipeline_mode=pl.Buffered(1)))


def _tok_prompt(x, gy, nb, l, w, w_uc, final):
    total = x.shape[0]
    tm = TOK_TM
    hist_rows = HIST * nb
    assert tm >= hist_rows and tm % CONV_TS == 0 and total % tm == 0
    ws = _tok_weight_specs(l)
    lc = functools.partial(_layer_const, l, pipeline_mode=pl.Buffered(1))
    n_tiles = total // tm
    tile = lambda width: pl.BlockSpec((tm, width), lambda n: (n, 0))
    tile_next = pl.BlockSpec((tm, D_MODEL), lambda n: (jnp.minimum(n + 1, n_tiles - 1), 0))
    return pl.pallas_call(
        functools.partial(_tok_prompt_kernel, tm=tm, nb=nb, final=final),
        grid=(n_tiles,),
        in_specs=[tile(D_MODEL), tile_next, tile(D_SSM), ws['gmix'],
                  lc((D_MODEL, D_CONV), index=(0, D_SSM // D_CONV)),
                  lc((D_MODEL, D_CONV), index=(0, D_SSM // D_CONV + 1)),
                  lc((CONV_W, D_CONV)), lc((1, D_CONV)), lc((1, D_CONV)), lc((1, D_CONV)),
                  *ws['wgate'], ws['wglu'], ws['wpw'], ws['wout'], ws['gffn'], ws['wffi'], ws['wffo'],
                  ws['gfin']],
        out_specs=[tile(D_MODEL), pl.BlockSpec((hist_rows, D_CONV), lambda n: (0, 0))],
        out_shape=[jax.ShapeDtypeStruct((total, D_MODEL), _F32),
                   jax.ShapeDtypeStruct((hist_rows, D_CONV), _F32)],
        scratch_shapes=[pltpu.VMEM((hist_rows + tm, D_CONV), _F32),
                        pltpu.VMEM((tm, D_CONV), _F32),
                        pltpu.VMEM((2, tm, D_MODEL), _BF),
                        pltpu.VMEM((2, tm, D_CONV), _BF),
                        pltpu.VMEM((tm, D_FF), _BF)],
        compiler_params=pltpu.CompilerParams(
            dimension_semantics=("arbitrary",), vmem_limit_bytes=VMEM_LIMIT),
        name=f"tok_prompt_l{l}",
    )(x, x, gy, w['g_mix'], w_uc, w_uc, w['conv_w'], w['conv_b'], w['ln_g'], w['ln_b'],
      *[w_uc] * GATE_BLOCKS, w['w_glu'], w['w_pw'], w['w_out'], w['g_ffn'], w['w_ff_in'], w['w_ff_out'],
      w['g_final'])


def _tok_sample_kernel(x_ref, gy_ref, cact_ref, gmix_ref, wg0_ref, wg1_ref, wg2_ref, wg3_ref, wglu_ref,
                       wpw_ref, wout_ref, gffn_ref, wffi_ref, wffo_ref, gfin_ref, o_ref, a_scr, *, final):
    x = x_ref[...]
    xn = _rms(x, gmix_ref[...]).astype(_BF)
    o_ref[...] = _merge_and_ffn(x, xn, cact_ref[...], gy_ref[...], (wg0_ref, wg1_ref, wg2_ref, wg3_ref),
                                wglu_ref, wpw_ref, wout_ref, gffn_ref, wffi_ref, wffo_ref, gfin_ref,
                                a_scr, final)


def _tok_sample(x, gy, cact, l, w, w_uc, final):
    nb = x.shape[0]
    ws = _tok_weight_specs(l)
    full = lambda shape: pl.BlockSpec(shape, lambda i: (0,) * len(shape))
    return pl.pallas_call(
        functools.partial(_tok_sample_kernel, final=final),
        grid=(1,),
        in_specs=[full((nb, D_MODEL)), full((nb, D_SSM)), full((nb, D_CONV)),
                  ws['gmix'], *ws['wgate'], ws['wglu'], ws['wpw'], ws['wout'], ws['gffn'], ws['wffi'],
                  ws['wffo'], ws['gfin']],
        out_specs=full((nb, D_MODEL)),
        out_shape=jax.ShapeDtypeStruct((nb, D_MODEL), _F32),
        scratch_shapes=[pltpu.VMEM((nb, D_FF), _BF)],
        compiler_params=pltpu.CompilerParams(
            dimension_semantics=("arbitrary",), vmem_limit_bytes=VMEM_LIMIT),
        name=f"tok_sample_l{l}",
    )(x, gy, cact, w['g_mix'], *[w_uc] * GATE_BLOCKS, w['w_glu'], w['w_pw'], w['w_out'], w['g_ffn'],
      w['w_ff_in'], w['w_ff_out'], w['g_final'])


def kernel(x_prompt, x_sample, state_ssm_re, state_ssm_im, state_conv, g_mix, w_in, ssm_a_re, ssm_a_im,
           ssm_log_dt, ssm_b_re, ssm_b_im, ssm_c_re, ssm_c_im, ssm_d, w_glu, conv_w, conv_b, conv_ln_g,
           conv_ln_b, w_pw, w_out, g_ffn, w_ff_in, w_ff_out, g_final):
    nbp = x_prompt.shape[0]
    nbs = x_sample.shape[0]
    row = lambda a: a.astype(_F32).reshape(DEPTH, 1, -1)
    s5 = _s5_params(ssm_a_re, ssm_a_im, ssm_log_dt, ssm_b_re, ssm_b_im, ssm_c_re, ssm_c_im)
    w_uc = w_in.astype(_BF)
    w = dict(
        g_mix=row(g_mix), w_glu=w_glu.astype(_BF),
        conv_w=conv_w.astype(_F32), conv_b=row(conv_b), ln_g=row(conv_ln_g), ln_b=row(conv_ln_b),
        w_pw=w_pw.astype(_BF), w_out=w_out.astype(_BF), g_ffn=row(g_ffn),
        w_ff_in=w_ff_in.astype(_BF), w_ff_out=w_ff_out.astype(_BF),
        g_final=g_final.astype(_F32).reshape(1, D_MODEL), ssm_d=row(ssm_d))

    seq = x_prompt.shape[1]
    xp = x_prompt.transpose(1, 0, 2).reshape(seq * nbp, D_MODEL)
    xs = x_sample.reshape(nbs, D_MODEL)
    h_re = state_ssm_re.reshape(DEPTH, nbs, N_STATE)
    h_im = state_ssm_im.reshape(DEPTH, nbs, N_STATE)
    buf = state_conv.transpose(0, 2, 1, 3)

    re_p, im_p, conv_p, re_s, im_s = [], [], [], [], []
    conv_s = jnp.zeros(buf.shape, _F32)
    for l in range(DEPTH):
        final = l == DEPTH - 1
        gy, hre, him = _seq_prompt(xp, nbp, l, w['g_mix'], w_uc, s5, w['ssm_d'])
        xp, hist = _tok_prompt(xp, gy, nbp, l, w, w_uc, final)
        re_p.append(hre.reshape(nbp, N_GROUPS, P_STATE))
        im_p.append(him.reshape(nbp, N_GROUPS, P_STATE))
        conv_p.append(hist.reshape(HIST, nbp, D_CONV)[HIST_PAD:].transpose(1, 0, 2))

        gys, cacts, nre, nim, conv_s = _seq_sample(xs, h_re, h_im, buf, conv_s, l, w, w_uc, s5)
        xs = _tok_sample(xs, gys, cacts, l, w, w_uc, final)
        re_s.append(nre.reshape(nbs, N_GROUPS, P_STATE))
        im_s.append(nim.reshape(nbs, N_GROUPS, P_STATE))

    y_prompt = xp.reshape(seq, nbp, D_MODEL).transpose(1, 0, 2)
    return (y_prompt, xs.reshape(nbs, 1, D_MODEL), jnp.stack(re_p), jnp.stack(im_p), jnp.stack(conv_p),
            jnp.stack(re_s), jnp.stack(im_s), conv_s.transpose(0, 2, 1, 3))
```

```python
import functools

import jax
import jax.numpy as jnp
from jax import lax
from jax.experimental import pallas as pl
from jax.experimental.pallas import tpu as pltpu

D_MODEL = 1024
DEPTH = 4
D_SSM = 512
GROUP = 16
N_GROUPS = 32
P_STATE = 64
D_CONV = 512
CONV_W = 31
D_FF = 2816
EPS = 1e-6

LANES = 128
SUBLANES = 8
MXU = 256
LC = 16
SLOTS = LANES // GROUP
STATE_W = 2 * P_STATE
N_STATE = N_GROUPS * P_STATE
HIST = 32
HIST_PAD = HIST - (CONV_W - 1)
FF_CHUNK = MXU
GATE_W = 512
GATE_COL0 = (D_SSM + 2 * D_CONV) // GATE_W
GATE_BLOCKS = 2 * D_MODEL // GATE_W
VMEM_LIMIT = 58 * 1024 * 1024

SEQ_TT = 256
SEQ_RB = 32
TOK_TM = 512
CONV_TS = 64

_BF = jnp.bfloat16
_F32 = jnp.float32


def _dot(a, b):
    return jnp.dot(a, b, preferred_element_type=_F32)


def _rms(xf, g):
    return xf * lax.rsqrt(jnp.mean(xf * xf, axis=-1, keepdims=True) + EPS) * g


def _sigmoid(x):
    return jax.nn.sigmoid(x)


def _layernorm_silu(c, g, b):
    mu = jnp.mean(c, axis=-1, keepdims=True)
    var = jnp.mean(jnp.square(c - mu), axis=-1, keepdims=True)
    y = (c - mu) * lax.rsqrt(var + EPS) * g + b
    return y * _sigmoid(y)


N_POW = LC + 1
ROW_PW1 = 0
ROW_PW2 = N_POW
ROW_Q1 = 2 * N_POW
ROW_Q2 = 2 * N_POW + 1
TAB_ROWS = 40


def _swap(v):
    return pltpu.roll(v, P_STATE, 1)


def _s5_ops_kernel(tab_ref, bt_ref, cx_ref, toe_ref, wsin_ref, wsout_ref):
    lane = lax.broadcasted_iota(jnp.int32, (GROUP, LANES), 1)
    zeros = jnp.zeros((GROUP, LANES), _F32)

    def one_group(g, g8):
        row = lambda r: tab_ref[g, pl.ds(r, 1), :]
        bt0 = bt_ref[g]
        bbar = bt0 * row(ROW_Q1) + _swap(bt0) * row(ROW_Q2)
        bbar_s = _swap(bbar)
        cx = cx_ref[g]
        cxs = _swap(cx)
        ca = [cx * row(ROW_PW1 + n) - cxs * row(ROW_PW2 + n) for n in range(N_POW)]
        kt = lax.dot_general(bbar, jnp.concatenate(ca[:LC], axis=0), (((1,), (1,)), ((), ())),
                             precision=lax.Precision.HIGHEST, preferred_element_type=_F32)
        kt0, kt1 = kt[:, :LANES], kt[:, LANES:]
        for j in range(LC):
            jh, jl = divmod(j, SLOTS)
            off = LANES * jh + GROUP * ((jl + g8) % SLOTS)
            r0 = pltpu.roll(kt0, GROUP * jl, 1) if jl else kt0
            r1 = pltpu.roll(kt1, GROUP * jl, 1) if jl else kt1
            lo = lane < GROUP * jl
            if jh == 0:
                o0 = jnp.where(lo, 0.0, r0) if jl else r0
                o1 = jnp.where(lo, r0, r1) if jl else r1
            else:
                o0 = zeros
                o1 = jnp.where(lo, 0.0, r0) if jl else r0
            if g8:
                o0 = pltpu.roll(o0, GROUP * g8, 1)
                o1 = pltpu.roll(o1, GROUP * g8, 1)
            toe_ref[g, pl.ds(off, GROUP), :] = jnp.concatenate([o0, o1], axis=1).astype(_BF)
            n = LC - 1 - j
            wsin_ref[g, pl.ds(off, GROUP), :] = (
                bbar * row(ROW_PW1 + n) + bbar_s * row(ROW_PW2 + n)).astype(_BF)
        wt = jnp.concatenate(ca[1:], axis=0).T
        w0, w1 = wt[:, :LANES], wt[:, LANES:]
        if g8:
            w0 = pltpu.roll(w0, GROUP * g8, 1)
            w1 = pltpu.roll(w1, GROUP * g8, 1)
        wsout_ref[g] = jnp.concatenate([w0, w1], axis=1).astype(_BF)

    def octet(q, carry):
        for g8 in range(SLOTS):
            one_group(q * SLOTS + g8, g8)
        return carry

    lax.fori_loop(0, N_GROUPS // SLOTS, octet, 0)


def _s5_params(a_re, a_im, log_dt, b_re, b_im, c_re, c_im):
    a_re, a_im = a_re.astype(_F32), a_im.astype(_F32)
    dt = jnp.exp(log_dt.astype(_F32))[..., None]
    lre, lim = dt * a_re, dt * a_im
    n = jnp.arange(LC + 1, dtype=_F32)[None, None, :, None]
    pmag = jnp.exp(n * lre[:, :, None, :])
    pw_re = pmag * jnp.cos(n * lim[:, :, None, :])
    pw_im = pmag * jnp.sin(n * lim[:, :, None, :])
    ab_re, ab_im = pw_re[:, :, 1], pw_im[:, :, 1]
    nr, ni = ab_re - 1.0, ab_im
    den = a_re * a_re + a_im * a_im
    q_re = (nr * a_re + ni * a_im) / den
    q_im = (ni * a_re - nr * a_im) / den
    both = lambda a, b: jnp.concatenate([a, b], axis=-1)
    pw1 = both(pw_re, pw_re)
    pw2 = both(-pw_im, pw_im)
    tab = jnp.concatenate(
        [pw1, pw2, both(q_re, q_re)[:, :, None], both(-q_im, q_im)[:, :, None],
         jnp.zeros((DEPTH, N_GROUPS, TAB_ROWS - 2 * N_POW - 2, STATE_W), _F32)], axis=2)
    bt0 = both(b_re.astype(_F32).transpose(0, 1, 3, 2), b_im.astype(_F32).transpose(0, 1, 3, 2))
    cx = both(c_re.astype(_F32), -c_im.astype(_F32))

    per_layer = lambda *shape: pl.BlockSpec((None,) + shape, lambda l: (l,) + (0,) * len(shape))
    toe, w_sin, w_sout = pl.pallas_call(
        _s5_ops_kernel,
        grid=(DEPTH,),
        in_specs=[per_layer(N_GROUPS, TAB_ROWS, STATE_W), per_layer(N_GROUPS, GROUP, STATE_W),
                  per_layer(N_GROUPS, GROUP, STATE_W)],
        out_specs=[per_layer(N_GROUPS, MXU, MXU), per_layer(N_GROUPS, MXU, STATE_W),
                   per_layer(N_GROUPS, STATE_W, MXU)],
        out_shape=[jax.ShapeDtypeStruct((DEPTH, N_GROUPS, MXU, MXU), _BF),
                   jax.ShapeDtypeStruct((DEPTH, N_GROUPS, MXU, STATE_W), _BF),
                   jax.ShapeDtypeStruct((DEPTH, N_GROUPS, STATE_W, MXU), _BF)],
        compiler_params=pltpu.CompilerParams(
            dimension_semantics=("arbitrary",), vmem_limit_bytes=VMEM_LIMIT),
        name="s5_operators",
    )(tab, bt0, cx)

    flat = lambda t: t.reshape(DEPTH, 1, N_GROUPS * STATE_W)
    mult_rows = lambda n: jnp.concatenate([flat(pw1[:, :, n]), flat(pw2[:, :, n])], axis=1)
    a_lc = jnp.stack([pw_re[:, :, LC].reshape(DEPTH, N_STATE), pw_im[:, :, LC].reshape(DEPTH, N_STATE)],
                     axis=1)
    return dict(toe=toe, w_sin=w_sin, w_sout=w_sout, a_lc=a_lc, a_1=mult_rows(1))


def _seq_prompt_kernel(x_ref, gmix_ref, wu_ref, toe_ref, wsin_ref, wsout_ref, alc_ref, d_ref,
                       gy_ref, hre_ref, him_ref,
                       u_scr, lhs_scr, e_scr, hs_scr, yg_scr, y_scr, hc_scr, *, tt):
    rows = x_ref.shape[0]
    nb = rows // tt
    m = rows // LC
    nk = tt // LC
    step_rows = lambda k, j: pl.ds((LC * k + j) * nb, nb)

    @pl.when(pl.program_id(0) == 0)
    def _():
        hc_scr[...] = jnp.zeros_like(hc_scr)

    xn = _rms(x_ref[...], gmix_ref[...]).astype(_BF)
    u_scr[...] = _dot(xn, wu_ref[...])

    slot = lax.broadcasted_iota(jnp.int32, (SEQ_RB, LANES), 1) // GROUP
    kb = SEQ_RB // nb

    for v in range(D_SSM // LANES):
        ls = pl.ds(LANES * v, LANES)
        for mb in range(m // SEQ_RB):
            mrows = pl.ds(SEQ_RB * mb, SEQ_RB)
            for hf in range(LC // SLOTS):
                rolled = []
                for jl in range(SLOTS):
                    piece = jnp.concatenate(
                        [u_scr[step_rows(kb * mb + k, SLOTS * hf + jl), ls] for k in range(kb)],
                        axis=0).astype(_BF)
                    rolled.append(pltpu.roll(piece, GROUP * jl, 1) if jl else piece)
                for gl in range(SLOTS):
                    acc = rolled[0]
                    for jl in range(1, SLOTS):
                        acc = jnp.where(slot == (gl + jl) % SLOTS, rolled[jl], acc)
                    lhs_scr[SLOTS * v + gl, mrows, pl.ds(LANES * hf, LANES)] = acc

    for g in range(N_GROUPS):
        e_scr[g] = _dot(lhs_scr[g], wsin_ref[g])

    lo = lax.broadcasted_iota(jnp.int32, (nb, LANES), 1) < P_STATE
    re_prev, im_prev = hc_scr[0], hc_scr[1]
    re_last, im_last = [], []
    for v in range(N_GROUPS // 2):
        sl = pl.ds(LANES * v, LANES)
        ar, ai = alc_ref[0:1, sl], alc_ref[1:2, sl]
        hr = re_prev[:, LANES * v:LANES * (v + 1)]
        hi = im_prev[:, LANES * v:LANES * (v + 1)]
        for k in range(nk):
            rws = pl.ds(nb * k, nb)
            e_even, e_odd = e_scr[2 * v, rws, :], e_scr[2 * v + 1, rws, :]
            e_re = jnp.where(lo, e_even, _swap(e_odd))
            e_im = jnp.where(lo, _swap(e_even), e_odd)
            hs_scr[2 * v, rws, :] = jnp.where(lo, hr, _swap(hi))
            hs_scr[2 * v + 1, rws, :] = jnp.where(lo, _swap(hr), hi)
            hr, hi = ar * hr - ai * hi + e_re, ar * hi + ai * hr + e_im
        re_last.append(hr)
        im_last.append(hi)
    re_last = jnp.concatenate(re_last, axis=1)
    im_last = jnp.concatenate(im_last, axis=1)
    hc_scr[0] = re_last
    hc_scr[1] = im_last
    hre_ref[...] = re_last
    him_ref[...] = im_last

    for g in range(N_GROUPS):
        yg_scr[g] = _dot(lhs_scr[g], toe_ref[g]) + _dot(hs_scr[g].astype(_BF), wsout_ref[g])

    for v in range(D_SSM // LANES):
        ls = pl.ds(LANES * v, LANES)
        dv = d_ref[:, ls]
        for mb in range(m // SEQ_RB):
            mrows = pl.ds(SEQ_RB * mb, SEQ_RB)
            for hf in range(LC // SLOTS):
                src = [yg_scr[SLOTS * v + gl, mrows, pl.ds(LANES * hf, LANES)] for gl in range(SLOTS)]
                for jl in range(SLOTS):
                    acc = src[0]
                    for gl in range(1, SLOTS):
                        acc = jnp.where(slot == (gl + jl) % SLOTS, src[gl], acc)
                    if jl:
                        acc = pltpu.roll(acc, GROUP * (SLOTS - jl), 1)
                    for k in range(kb):
                        rws = step_rows(kb * mb + k, SLOTS * hf + jl)
                        y_scr[rws, ls] = acc[nb * k:nb * (k + 1)] + dv * u_scr[rws, ls]

    gy_ref[...] = jax.nn.gelu(y_scr[...]).astype(_BF)


def _layer_const(l, shape, index=None, **kw):
    idx = (l,) + (tuple(index) if index is not None else (0,) * len(shape))
    return pl.BlockSpec((None,) + tuple(shape), lambda *_: idx, **kw)


def _seq_prompt(x, nb, l, gmix, w_uc, s5, ssm_d):
    assert nb == SUBLANES
    seq = x.shape[0] // nb
    tt = SEQ_TT
    rows = nb * tt
    m = rows // LC
    lc = functools.partial(_layer_const, l, pipeline_mode=pl.Buffered(1))
    return pl.pallas_call(
        functools.partial(_seq_prompt_kernel, tt=tt),
        grid=(seq // tt,),
        in_specs=[
            pl.BlockSpec((rows, D_MODEL), lambda i: (i, 0)),
            lc((1, D_MODEL)),
            lc((D_MODEL, D_SSM)),
            lc((N_GROUPS, MXU, MXU)),
            lc((N_GROUPS, MXU, STATE_W)),
            lc((N_GROUPS, STATE_W, MXU)),
            lc((2, N_STATE)),
            lc((1, D_SSM)),
        ],
        out_specs=[
            pl.BlockSpec((rows, D_SSM), lambda i: (i, 0)),
            pl.BlockSpec((nb, N_STATE), lambda i: (0, 0)),
            pl.BlockSpec((nb, N_STATE), lambda i: (0, 0)),
        ],
        out_shape=[
            jax.ShapeDtypeStruct((seq * nb, D_SSM), _BF),
            jax.ShapeDtypeStruct((nb, N_STATE), _F32),
            jax.ShapeDtypeStruct((nb, N_STATE), _F32),
        ],
        scratch_shapes=[
            pltpu.VMEM((rows, D_SSM), _F32),
            pltpu.VMEM((N_GROUPS, m, MXU), _BF),
            pltpu.VMEM((N_GROUPS, m, STATE_W), _F32),
            pltpu.VMEM((N_GROUPS, m, STATE_W), _F32),
            pltpu.VMEM((N_GROUPS, m, MXU), _F32),
            pltpu.VMEM((rows, D_SSM), _F32),
            pltpu.VMEM((2, nb, N_STATE), _F32),
        ],
        compiler_params=pltpu.CompilerParams(
            dimension_semantics=("arbitrary",), vmem_limit_bytes=VMEM_LIMIT),
        name=f"seq_prompt_l{l}",
    )(x, gmix, w_uc, s5['toe'], s5['w_sin'], s5['w_sout'], s5['a_lc'], ssm_d)


def _seq_sample_kernel(x_ref, hre_ref, him_ref, buf_ref, gmix_ref, wuc_ref, toe_ref, wsin_ref, wsout_ref,
                       a1_ref, d_ref, cw_ref, cb_ref, lng_ref, lnb_ref, nbuf_all_ref,
                       gy_ref, cact_ref, nre_ref, nim_ref, nbuf_ref):
    del nbuf_all_ref
    nb = x_ref.shape[0]
    xn = _rms(x_ref[...], gmix_ref[...]).astype(_BF)
    z = _dot(xn, wuc_ref[...])
    u = z[:, :D_SSM]
    c = z[:, D_SSM:D_SSM + D_CONV] * _sigmoid(z[:, D_SSM + D_CONV:])

    lane = lax.broadcasted_iota(jnp.int32, (nb, LANES), 1)
    slot = lane // GROUP
    lo = lane < P_STATE
    zero = jnp.zeros((nb, LANES), _F32)
    last = LC - 1

    ys = []
    for v in range(D_SSM // LANES):
        uv = u[:, LANES * v:LANES * (v + 1)]
        u_last = pltpu.roll(uv, GROUP * (last % SLOTS), 1)
        yv = None
        for gp in range(SLOTS // 2):
            sv = (SLOTS * v) // 2 + gp
            rv = hre_ref[:, pl.ds(LANES * sv, LANES)]
            iv = him_ref[:, pl.ds(LANES * sv, LANES)]
            h0s = (jnp.where(lo, rv, pltpu.roll(iv, P_STATE, 1)),
                   jnp.where(lo, pltpu.roll(rv, P_STATE, 1), iv))
            hn = []
            for par in range(2):
                gl = 2 * gp + par
                g = SLOTS * v + gl
                h0 = h0s[par]
                lhs_l = jnp.concatenate(
                    [zero, jnp.where(slot == (gl + last) % SLOTS, u_last, 0.0)], axis=1).astype(_BF)
                e = _dot(lhs_l, wsin_ref[g])
                sl = pl.ds(STATE_W * g, STATE_W)
                hn.append(h0 * a1_ref[0:1, sl] + pltpu.roll(h0, P_STATE, 1) * a1_ref[1:2, sl] + e)
                lhs_f = jnp.concatenate([jnp.where(slot == gl, uv, 0.0), zero], axis=1).astype(_BF)
                yg = (_dot(lhs_f, toe_ref[g, :, pl.ds(0, LANES)])
                      + _dot(h0.astype(_BF), wsout_ref[g, :, pl.ds(0, LANES)]))
                yv = yg if yv is None else jnp.where(slot == gl, yg, yv)
            nre_ref[:, pl.ds(LANES * sv, LANES)] = jnp.where(lo, hn[0], pltpu.roll(hn[1], P_STATE, 1))
            nim_ref[:, pl.ds(LANES * sv, LANES)] = jnp.where(lo, pltpu.roll(hn[0], P_STATE, 1), hn[1])
        ys.append(yv + d_ref[:, pl.ds(LANES * v, LANES)] * uv)
    gy_ref[...] = jax.nn.gelu(jnp.concatenate(ys, axis=1)).astype(_BF)

    acc = cb_ref[...] + cw_ref[pl.ds(CONV_W - 1, 1), :] * c
    for k in range(CONV_W - 1):
        tap = buf_ref[k]
        acc = acc + cw_ref[pl.ds(k, 1), :] * tap
        if k:
            nbuf_ref[k - 1] = tap
    nbuf_ref[CONV_W - 2] = c
    cact_ref[...] = _layernorm_silu(acc, lng_ref[...], lnb_ref[...]).astype(_BF)


def _seq_sample(x, h_re, h_im, buf, new_buf, l, w, w_uc, s5):
    nb = x.shape[0]
    hist = (CONV_W - 1, nb, D_CONV)
    one = pl.Buffered(1)
    full = lambda shape: pl.BlockSpec(shape, lambda i: (0,) * len(shape))
    lc = functools.partial(_layer_const, l, pipeline_mode=one)
    in_specs = [
        full((nb, D_MODEL)), lc((nb, N_STATE)), lc((nb, N_STATE)), lc(hist),
        lc((1, D_MODEL)), lc((D_MODEL, D_SSM + 2 * D_CONV)),
        lc((N_GROUPS, MXU, MXU)), lc((N_GROUPS, MXU, STATE_W)), lc((N_GROUPS, STATE_W, MXU)),
        lc((2, N_GROUPS * STATE_W)), lc((1, D_SSM)),
        lc((CONV_W, D_CONV)), lc((1, D_CONV)), lc((1, D_CONV)), lc((1, D_CONV)),
        pl.BlockSpec(memory_space=pl.ANY),
    ]
    return pl.pallas_call(
        _seq_sample_kernel,
        grid=(1,),
        in_specs=in_specs,
        out_specs=[full((nb, D_SSM)), full((nb, D_CONV)), full((nb, N_STATE)), full((nb, N_STATE)),
                   _layer_const(l, hist)],
        out_shape=[
            jax.ShapeDtypeStruct((nb, D_SSM), _BF),
            jax.ShapeDtypeStruct((nb, D_CONV), _BF),
            jax.ShapeDtypeStruct((nb, N_STATE), _F32),
            jax.ShapeDtypeStruct((nb, N_STATE), _F32),
            jax.ShapeDtypeStruct((DEPTH,) + hist, _F32),
        ],
        input_output_aliases={len(in_specs) - 1: 4},
        compiler_params=pltpu.CompilerParams(
            dimension_semantics=("arbitrary",), vmem_limit_bytes=VMEM_LIMIT),
        name=f"seq_sample_l{l}",
    )(x, h_re, h_im, buf, w['g_mix'], w_uc, s5['toe'], s5['w_sin'], s5['w_sout'], s5['a_1'], w['ssm_d'],
      w['conv_w'], w['conv_b'], w['ln_g'], w['ln_b'], new_buf)


def _merge_and_ffn(x, xn, cact, gy, wgate_refs, wglu_ref, wpw_ref, wout_ref, gffn_ref,
                   wffi_ref, wffo_ref, gfin_ref, a_scr, final, side_work=()):
    yg = _dot(gy, wglu_ref[...])
    ya = yg[:, :D_MODEL] * _sigmoid(yg[:, D_MODEL:])
    yb = _dot(cact, wpw_ref[...])
    gates = [_sigmoid(_dot(xn, r[...])) for r in wgate_refs]
    ga = jnp.concatenate(gates[:GATE_BLOCKS // 2], axis=1)
    gb = jnp.concatenate(gates[GATE_BLOCKS // 2:], axis=1)
    merged = (ga * ya + gb * yb).astype(_BF)
    x1 = x + _dot(merged, wout_ref[...])
    hn = _rms(x1, gffn_ref[...]).astype(_BF)
    n_chunks = D_FF // FF_CHUNK
    per_chunk = -(-len(side_work) // n_chunks)
    for ck in range(n_chunks):
        h1 = _dot(hn, wffi_ref[:, pl.ds(FF_CHUNK * ck, FF_CHUNK)])
        h2 = _dot(hn, wffi_ref[:, pl.ds(D_FF + FF_CHUNK * ck, FF_CHUNK)])
        a_scr[:, pl.ds(FF_CHUNK * ck, FF_CHUNK)] = (h1 * _sigmoid(h1) * h2).astype(_BF)
        for thunk in side_work[per_chunk * ck:per_chunk * (ck + 1)]:
            thunk()
    acc = x1 + _dot(a_scr[...], wffo_ref[...])
    if final:
        acc = _rms(acc, gfin_ref[...])
    return acc


def _conv_head(x, gmix_ref, wcv_ref, wcg_ref, cfull_scr, tm, hist_rows):
    xn = _rms(x, gmix_ref[...]).astype(_BF)
    cfull_scr[pl.ds(hist_rows, tm), :] = _dot(xn, wcv_ref[...]) * _sigmoid(_dot(xn, wcg_ref[...]))
    return xn


def _conv_tiles(cw_ref, cb_ref, cfull_scr, conv_scr, tm, nb):
    def tile(t0, lb):
        ls = pl.ds(LANES * lb, LANES)
        acc = jnp.broadcast_to(cb_ref[:, ls], (CONV_TS, LANES))
        for k in range(CONV_W):
            acc = acc + cw_ref[pl.ds(k, 1), ls] * cfull_scr[pl.ds(t0 + nb * (k + HIST_PAD), CONV_TS), ls]
        conv_scr[pl.ds(t0, CONV_TS), ls] = acc
    return [functools.partial(tile, ts * CONV_TS, lb)
            for ts in range(tm // CONV_TS) for lb in range(D_CONV // LANES)]


def _conv_tail(lng_ref, lnb_ref, cfull_scr, conv_scr, tm, hist_rows):
    cact = _layernorm_silu(conv_scr[...], lng_ref[...], lnb_ref[...]).astype(_BF)
    tail = cfull_scr[pl.ds(tm, hist_rows), :]
    cfull_scr[pl.ds(0, hist_rows), :] = tail
    return cact, tail


def _tok_prompt_kernel(x_ref, xnext_ref, gy_ref, gmix_ref, wcv_ref, wcg_ref, cw_ref, cb_ref, lng_ref,
                       lnb_ref, wg0_ref, wg1_ref, wg2_ref, wg3_ref, wglu_ref, wpw_ref, wout_ref, gffn_ref,
                       wffi_ref, wffo_ref, gfin_ref, o_ref, hist_ref, cfull_scr, conv_scr, xn_scr, cact_scr, a_scr,
                       *, tm, nb, final):
    n = pl.program_id(0)
    hist_rows = HIST * nb
    head = functools.partial(_conv_head, gmix_ref=gmix_ref, wcv_ref=wcv_ref, wcg_ref=wcg_ref,
                             cfull_scr=cfull_scr, tm=tm, hist_rows=hist_rows)
    tiles = _conv_tiles(cw_ref, cb_ref, cfull_scr, conv_scr, tm, nb)
    tail = functools.partial(_conv_tail, lng_ref, lnb_ref, cfull_scr, conv_scr, tm, hist_rows)

    @pl.when(n == 0)
    def _():
        cfull_scr[pl.ds(0, hist_rows), :] = jnp.zeros((hist_rows, D_CONV), _F32)
        xn_scr[0] = head(x_ref[...])
        for thunk in tiles:
            thunk()
        cact_scr[0] = tail()[0]

    cur = n % 2
    xn_cur = xn_scr[cur]
    cact_cur = cact_scr[cur]
    xn_scr[1 - cur] = head(xnext_ref[...])
    o_ref[...] = _merge_and_ffn(x_ref[...], xn_cur, cact_cur, gy_ref[...],
                                (wg0_ref, wg1_ref, wg2_ref, wg3_ref), wglu_ref, wpw_ref, wout_ref,
                                gffn_ref, wffi_ref, wffo_ref, gfin_ref, a_scr, final, side_work=tiles)
    cact_next, last_rows = tail()
    cact_scr[1 - cur] = cact_next
    hist_ref[...] = last_rows


def _tok_weight_specs(l):
    lc = functools.partial(_layer_const, l, pipeline_mode=pl.Buffered(1))
    return dict(
        gmix=lc((1, D_MODEL)),
        wgate=[lc((D_MODEL, GATE_W), index=(0, GATE_COL0 + i)) for i in range(GATE_BLOCKS)],
        wglu=lc((D_SSM, 2 * D_MODEL)), wpw=lc((D_CONV, D_MODEL)),
        wout=lc((D_MODEL, D_MODEL)), gffn=lc((1, D_MODEL)), wffi=lc((D_MODEL, 2 * D_FF)),
        wffo=lc((D_FF, D_MODEL)),
        gfin=pl.BlockSpec((1, D_MODEL), lambda *_: (0, 0), pipeline_mode=pl.Buffered(1)))


def _tok_prompt(x, gy, nb, l, w, w_uc, final):
    total = x.shape[0]
    tm = TOK_TM
    hist_rows = HIST * nb
    assert tm >= hist_rows and tm % CONV_TS == 0 and total % tm == 0
    ws = _tok_weight_specs(l)
    lc = functools.partial(_layer_const, l, pipeline_mode=pl.Buffered(1))
    n_tiles = total // tm
    tile = lambda width: pl.BlockSpec((tm, width), lambda n: (n, 0))
    tile_next = pl.BlockSpec((tm, D_MODEL), lambda n: (jnp.minimum(n + 1, n_tiles - 1), 0))
    return pl.pallas_call(
        functools.partial(_tok_prompt_kernel, tm=tm, nb=nb, final=final),
        grid=(n_tiles,),
        in_specs=[tile(D_MODEL), tile_next, tile(D_SSM), ws['gmix'],
                  lc((D_MODEL, D_CONV), index=(0, D_SSM // D_CONV)),
                  lc((D_MODEL, D_CONV), index=(0, D_SSM // D_CONV + 1)),
                  lc((CONV_W, D_CONV)), lc((1, D_CONV)), lc((1, D_CONV)), lc((1, D_CONV)),
                  *ws['wgate'], ws['wglu'], ws['wpw'], ws['wout'], ws['gffn'], ws['wffi'], ws['wffo'],
                  ws['gfin']],
        out_specs=[tile(D_MODEL), pl.BlockSpec((hist_rows, D_CONV), lambda n: (0, 0))],
        out_shape=[jax.ShapeDtypeStruct((total, D_MODEL), _F32),
                   jax.ShapeDtypeStruct((hist_rows, D_CONV), _F32)],
        scratch_shapes=[pltpu.VMEM((hist_rows + tm, D_CONV), _F32),
                        pltpu.VMEM((tm, D_CONV), _F32),
                        pltpu.VMEM((2, tm, D_MODEL), _BF),
                        pltpu.VMEM((2, tm, D_CONV), _BF),
                        pltpu.VMEM((tm, D_FF), _BF)],
        compiler_params=pltpu.CompilerParams(
            dimension_semantics=("arbitrary",), vmem_limit_bytes=VMEM_LIMIT),
        name=f"tok_prompt_l{l}",
    )(x, x, gy, w['g_mix'], w_uc, w_uc, w['conv_w'], w['conv_b'], w['ln_g'], w['ln_b'],
      *[w_uc] * GATE_BLOCKS, w['w_glu'], w['w_pw'], w['w_out'], w['g_ffn'], w['w_ff_in'], w['w_ff_out'],
      w['g_final'])


def _tok_sample_kernel(x_ref, gy_ref, cact_ref, gmix_ref, wg0_ref, wg1_ref, wg2_ref, wg3_ref, wglu_ref,
                       wpw_ref, wout_ref, gffn_ref, wffi_ref, wffo_ref, gfin_ref, o_ref, a_scr, *, final):
    x = x_ref[...]
    xn = _rms(x, gmix_ref[...]).astype(_BF)
    o_ref[...] = _merge_and_ffn(x, xn, cact_ref[...], gy_ref[...], (wg0_ref, wg1_ref, wg2_ref, wg3_ref),
                                wglu_ref, wpw_ref, wout_ref, gffn_ref, wffi_ref, wffo_ref, gfin_ref,
                                a_scr, final)


def _tok_sample(x, gy, cact, l, w, w_uc, final):
    nb = x.shape[0]
    ws = _tok_weight_specs(l)
    full = lambda shape: pl.BlockSpec(shape, lambda i: (0,) * len(shape))
    return pl.pallas_call(
        functools.partial(_tok_sample_kernel, final=final),
        grid=(1,),
        in_specs=[full((nb, D_MODEL)), full((nb, D_SSM)), full((nb, D_CONV)),
                  ws['gmix'], *ws['wgate'], ws['wglu'], ws['wpw'], ws['wout'], ws['gffn'], ws['wffi'],
                  ws['wffo'], ws['gfin']],
        out_specs=full((nb, D_MODEL)),
        out_shape=jax.ShapeDtypeStruct((nb, D_MODEL), _F32),
        scratch_shapes=[pltpu.VMEM((nb, D_FF), _BF)],
        compiler_params=pltpu.CompilerParams(
            dimension_semantics=("arbitrary",), vmem_limit_bytes=VMEM_LIMIT),
        name=f"tok_sample_l{l}",
    )(x, gy, cact, w['g_mix'], *[w_uc] * GATE_BLOCKS, w['w_glu'], w['w_pw'], w['w_out'], w['g_ffn'],
      w['w_ff_in'], w['w_ff_out'], w['g_final'])


def kernel(x_prompt, x_sample, state_ssm_re, state_ssm_im, state_conv, g_mix, w_in, ssm_a_re, ssm_a_im,
           ssm_log_dt, ssm_b_re, ssm_b_im, ssm_c_re, ssm_c_im, ssm_d, w_glu, conv_w, conv_b, conv_ln_g,
           conv_ln_b, w_pw, w_out, g_ffn, w_ff_in, w_ff_out, g_final):
    nbp = x_prompt.shape[0]
    nbs = x_sample.shape[0]
    row = lambda a: a.astype(_F32).reshape(DEPTH, 1, -1)
    s5 = _s5_params(ssm_a_re, ssm_a_im, ssm_log_dt, ssm_b_re, ssm_b_im, ssm_c_re, ssm_c_im)
    w_uc = w_in.astype(_BF)
    w = dict(
        g_mix=row(g_mix), w_glu=w_glu.astype(_BF),
        conv_w=conv_w.astype(_F32), conv_b=row(conv_b), ln_g=row(conv_ln_g), ln_b=row(conv_ln_b),
        w_pw=w_pw.astype(_BF), w_out=w_out.astype(_BF), g_ffn=row(g_ffn),
        w_ff_in=w_ff_in.astype(_BF), w_ff_out=w_ff_out.astype(_BF),
        g_final=g_final.astype(_F32).reshape(1, D_MODEL), ssm_d=row(ssm_d))

    seq = x_prompt.shape[1]
    xp = x_prompt.transpose(1, 0, 2).reshape(seq * nbp, D_MODEL)
    xs = x_sample.reshape(nbs, D_MODEL)
    h_re = state_ssm_re.reshape(DEPTH, nbs, N_STATE)
    h_im = state_ssm_im.reshape(DEPTH, nbs, N_STATE)
    buf = state_conv.transpose(0, 2, 1, 3)

    re_p, im_p, conv_p, re_s, im_s = [], [], [], [], []
    conv_s = jnp.zeros(buf.shape, _F32)
    for l in range(DEPTH):
        final = l == DEPTH - 1
        gy, hre, him = _seq_prompt(xp, nbp, l, w['g_mix'], w_uc, s5, w['ssm_d'])
        xp, hist = _tok_prompt(xp, gy, nbp, l, w, w_uc, final)
        re_p.append(hre.reshape(nbp, N_GROUPS, P_STATE))
        im_p.append(him.reshape(nbp, N_GROUPS, P_STATE))
        conv_p.append(hist.reshape(HIST, nbp, D_CONV)[HIST_PAD:].transpose(1, 0, 2))

        gys, cacts, nre, nim, conv_s = _seq_sample(xs, h_re, h_im, buf, conv_s, l, w, w_uc, s5)
        xs = _tok_sample(xs, gys, cacts, l, w, w_uc, final)
        re_s.append(nre.reshape(nbs, N_GROUPS, P_STATE))
        im_s.append(nim.reshape(nbs, N_GROUPS, P_STATE))

    y_prompt = xp.reshape(seq, nbp, D_MODEL).transpose(1, 0, 2)
    return (y_prompt, xs.reshape(nbs, 1, D_MODEL), jnp.stack(re_p), jnp.stack(im_p), jnp.stack(conv_p),
            jnp.stack(re_s), jnp.stack(im_s), conv_s.transpose(0, 2, 1, 3))
```
